```python
import math
import jax, jax.numpy as jnp
from jax import lax
import numpy as np

D_MODEL = 1024
BATCH = 16
SEQ = 2048
DEPTH = 1

W_S5 = D_MODEL // 2
S5_GROUP_CH = 16
S5_GROUPS = W_S5 // S5_GROUP_CH
S5_STATE = 64
W_LRU = D_MODEL - W_S5
LRU_HEADS = 8
LRU_HEAD_DIM = W_LRU // LRU_HEADS
CONV_WIDTH = 4
RG_C = 8.0
D_FF = int(math.ceil(8 * D_MODEL / 3 / 256) * 256)
D_IN = W_S5 + 2 * W_LRU
EPS = 1e-6

kernel_name = "hybrid_s5_rglru_parallel_heads"


def rms_norm(x, g):
    xf = x.astype(jnp.float32)
    ms = jnp.mean(xf * xf, axis=-1, keepdims=True)
    return (xf * lax.rsqrt(ms + EPS) * g.astype(jnp.float32)).astype(x.dtype)


def _linear_combine(p, q):
    a_i, b_i = p
    a_j, b_j = q
    return a_j * a_i, a_j * b_i + b_j


def s5_mixer(u, lam_re, lam_im, log_step, b_re, b_im, c_re, c_im, d, w_glu, b_glu):
    bsz, seq, _ = u.shape
    f32 = jnp.float32
    uf = u.astype(f32).reshape(bsz, seq, S5_GROUPS, S5_GROUP_CH)
    lam = lax.complex(jnp.minimum(lam_re.astype(f32), -1e-4), lam_im.astype(f32))
    step = jnp.exp(log_step.astype(f32))[:, None]
    lam_bar = jnp.exp(lam * step)
    b = lax.complex(b_re.astype(f32), b_im.astype(f32))
    b_bar = ((lam_bar - 1.0) / lam)[..., None] * b
    bu = lax.complex(jnp.einsum('blgc,gnc->blgn', uf, jnp.real(b_bar)),
                     jnp.einsum('blgc,gnc->blgn', uf, jnp.imag(b_bar)))
    lam_elems = jnp.broadcast_to(lam_bar, (seq, S5_GROUPS, S5_STATE))
    states = jax.vmap(lambda e: lax.associative_scan(_linear_combine, (lam_elems, e))[1])(bu)
    y = (jnp.einsum('blgn,gcn->blgc', jnp.real(states), c_re.astype(f32))
         - jnp.einsum('blgn,gcn->blgc', jnp.imag(states), c_im.astype(f32))
         + d.astype(f32) * uf)
    y = y.reshape(bsz, seq, W_S5)
    z = jax.nn.gelu(y)
    z = z * jax.nn.sigmoid(z @ w_glu.astype(f32) + b_glu.astype(f32))
    return z.astype(u.dtype)


def rglru_mixer(xr, yr, conv_w, conv_b, w_a, b_a, w_x, b_x, lam):
    bsz, seq, _ = xr.shape
    f32 = jnp.float32
    xf = xr.astype(f32)
    xpad = jnp.pad(xf, ((0, 0), (CONV_WIDTH - 1, 0), (0, 0)))
    cw = conv_w.astype(f32)
    conv = conv_b.astype(f32) + sum(xpad[:, k:k + seq, :] * cw[k] for k in range(CONV_WIDTH))
    xh = conv.reshape(bsz, seq, LRU_HEADS, LRU_HEAD_DIM)
    r = jax.nn.sigmoid(jnp.einsum('blhi,hij->blhj', xh, w_a.astype(f32)) + b_a.astype(f32))
    i = jax.nn.sigmoid(jnp.einsum('blhi,hij->blhj', xh, w_x.astype(f32)) + b_x.astype(f32))
    log_a = RG_C * r * jax.nn.log_sigmoid(lam.astype(f32))
    a = jnp.exp(log_a)
    gated = jnp.sqrt(-jnp.expm1(2.0 * log_a)) * (i * xh)
    _, h = lax.associative_scan(_linear_combine, (a, gated), axis=1)
    out = h.reshape(bsz, seq, W_LRU) * jax.nn.gelu(yr.astype(f32))
    return out.astype(xr.dtype)


def setup_inputs(seed: int = 0) -> dict:
    key = jax.random.key(seed)
    ks = jax.random.split(key, 32)
    f32 = jnp.float32
    nrm = lambda k, shape, std: std * jax.random.normal(k, shape, f32)
    x = jax.random.normal(ks[0], (BATCH, SEQ, D_MODEL), f32)
    norm1_g = 1.0 + nrm(ks[1], (DEPTH, D_MODEL), 0.02)
    w_in = nrm(ks[2], (DEPTH, D_MODEL, D_IN), D_MODEL ** -0.5)
    n_idx = jnp.arange(S5_STATE, dtype=f32)
    s5_lambda_re = -0.5 + nrm(ks[3], (DEPTH, S5_GROUPS, S5_STATE), 0.01)
    s5_lambda_im = math.pi * n_idx + nrm(ks[4], (DEPTH, S5_GROUPS, S5_STATE), 0.01)
    s5_log_step = jax.random.uniform(ks[5], (DEPTH, S5_GROUPS), f32, math.log(1e-3), math.log(1e-1))
    bstd = (S5_GROUP_CH ** -0.5) / math.sqrt(2.0)
    s5_b_re = nrm(ks[6], (DEPTH, S5_GROUPS, S5_STATE, S5_GROUP_CH), bstd)
    s5_b_im = nrm(ks[7], (DEPTH, S5_GROUPS, S5_STATE, S5_GROUP_CH), bstd)
    cstd = (S5_STATE ** -0.5) / math.sqrt(2.0)
    s5_c_re = nrm(ks[8], (DEPTH, S5_GROUPS, S5_GROUP_CH, S5_STATE), cstd)
    s5_c_im = nrm(ks[9], (DEPTH, S5_GROUPS, S5_GROUP_CH, S5_STATE), cstd)
    s5_d = nrm(ks[10], (DEPTH, S5_GROUPS, S5_GROUP_CH), 1.0)
    s5_w_glu = nrm(ks[11], (DEPTH, W_S5, W_S5), W_S5 ** -0.5)
    s5_b_glu = nrm(ks[12], (DEPTH, W_S5), 0.01)
    lru_conv_w = nrm(ks[13], (DEPTH, CONV_WIDTH, W_LRU), CONV_WIDTH ** -0.5)
    lru_conv_b = nrm(ks[14], (DEPTH, W_LRU), 0.01)
    lru_w_a = nrm(ks[15], (DEPTH, LRU_HEADS, LRU_HEAD_DIM, LRU_HEAD_DIM), LRU_HEAD_DIM ** -0.5)
    lru_b_a = nrm(ks[16], (DEPTH, LRU_HEADS, LRU_HEAD_DIM), 0.01)
    lru_w_x = nrm(ks[17], (DEPTH, LRU_HEADS, LRU_HEAD_DIM, LRU_HEAD_DIM), LRU_HEAD_DIM ** -0.5)
    lru_b_x = nrm(ks[18], (DEPTH, LRU_HEADS, LRU_HEAD_DIM), 0.01)
    a_c = jax.random.uniform(ks[19], (DEPTH, LRU_HEADS, LRU_HEAD_DIM), f32, 0.9, 0.999)
    a0 = a_c ** (1.0 / RG_C)
    lru_lambda = jnp.log(a0) - jnp.log1p(-a0)
    s5_out_g = 1.0 + nrm(ks[20], (DEPTH, W_S5), 0.02)
    lru_out_g = 1.0 + nrm(ks[21], (DEPTH, W_LRU), 0.02)
    w_out = nrm(ks[22], (DEPTH, D_MODEL, D_MODEL), D_MODEL ** -0.5)
    norm2_g = 1.0 + nrm(ks[23], (DEPTH, D_MODEL), 0.02)
    w_gate = nrm(ks[24], (DEPTH, D_MODEL, D_FF), D_MODEL ** -0.5)
    w_up = nrm(ks[25], (DEPTH, D_MODEL, D_FF), D_MODEL ** -0.5)
    w_down = nrm(ks[26], (DEPTH, D_FF, D_MODEL), D_FF ** -0.5)
    final_g = 1.0 + nrm(ks[27], (D_MODEL,), 0.02)
    return {"x": x, "norm1_g": norm1_g, "w_in": w_in,
            "s5_lambda_re": s5_lambda_re, "s5_lambda_im": s5_lambda_im, "s5_log_step": s5_log_step,
            "s5_b_re": s5_b_re, "s5_b_im": s5_b_im, "s5_c_re": s5_c_re, "s5_c_im": s5_c_im,
            "s5_d": s5_d, "s5_w_glu": s5_w_glu, "s5_b_glu": s5_b_glu,
            "lru_conv_w": lru_conv_w, "lru_conv_b": lru_conv_b, "lru_w_a": lru_w_a, "lru_b_a": lru_b_a,
            "lru_w_x": lru_w_x, "lru_b_x": lru_b_x, "lru_lambda": lru_lambda,
            "s5_out_g": s5_out_g, "lru_out_g": lru_out_g, "w_out": w_out,
            "norm2_g": norm2_g, "w_gate": w_gate, "w_up": w_up, "w_down": w_down, "final_g": final_g}


def reference(x, norm1_g, w_in, s5_lambda_re, s5_lambda_im, s5_log_step, s5_b_re, s5_b_im,
              s5_c_re, s5_c_im, s5_d, s5_w_glu, s5_b_glu, lru_conv_w, lru_conv_b, lru_w_a, lru_b_a,
              lru_w_x, lru_b_x, lru_lambda, s5_out_g, lru_out_g, w_out, norm2_g, w_gate, w_up,
              w_down, final_g):
    for l in range(DEPTH):
        h = rms_norm(x, norm1_g[l])
        proj = h @ w_in[l]
        u = proj[..., :W_S5]
        xr = proj[..., W_S5:W_S5 + W_LRU]
        yr = proj[..., W_S5 + W_LRU:]
        s5_out = s5_mixer(u, s5_lambda_re[l], s5_lambda_im[l], s5_log_step[l], s5_b_re[l], s5_b_im[l],
                          s5_c_re[l], s5_c_im[l], s5_d[l], s5_w_glu[l], s5_b_glu[l])
        lru_out = rglru_mixer(xr, yr, lru_conv_w[l], lru_conv_b[l], lru_w_a[l], lru_b_a[l],
                              lru_w_x[l], lru_b_x[l], lru_lambda[l])
        mix = jnp.concatenate([rms_norm(s5_out, s5_out_g[l]), rms_norm(lru_out, lru_out_g[l])], axis=-1)
        x = x + mix @ w_out[l]
        h2 = rms_norm(x, norm2_g[l])
        x = x + (jax.nn.silu(h2 @ w_gate[l]) * (h2 @ w_up[l])) @ w_down[l]
    return rms_norm(x, final_g)
```

```python
import functools
import math

import jax
import jax.numpy as jnp
from jax import lax
from jax.experimental import pallas as pl
from jax.experimental.pallas import tpu as pltpu

D_MODEL = 1024
BATCH = 16
SEQ = 2048
W_S5 = 512
S5_GROUP_CH = 16
S5_GROUPS = 32
S5_STATE = 64
W_LRU = 512
LRU_HEADS = 8
LRU_HEAD_DIM = 64
CONV_WIDTH = 4
RG_C = 8.0
D_FF = 2816
EPS = 1e-6

MXU_TILE = 256
S5_HALVES = W_S5 // MXU_TILE
HALF_GROUPS = S5_GROUPS // S5_HALVES
HALF_STATES = HALF_GROUPS * S5_STATE
SCAN_SLAB = 512
LRU_HALF_HEADS = LRU_HEADS // 2

TIME_CHUNK = 16
MIX_ROWS = TIME_CHUNK * BATCH
HIST_ROWS = (CONV_WIDTH - 1) * BATCH
LANES = 128
LANE_CHUNKS = D_MODEL // LANES
CHUNK_SEG = TIME_CHUNK * LANE_CHUNKS
FFN_ROWS = 512
FF_CHUNK = 256
VMEM_LIMIT_BYTES = 56 * 1024 * 1024


def _rms(x, g):
    ms = jnp.mean(x * x, axis=-1, keepdims=True)
    return x * lax.rsqrt(ms + EPS) * g


def _bdot(a, b):
    return jnp.dot(a.astype(jnp.bfloat16), b, preferred_element_type=jnp.float32)


def _mixer_kernel(x_ref, g1_ref, win_ref, wb_ref, lam_ref, wc_ref, d_ref, wglu_ref, bglu_ref,
                  cw_ref, cb_ref, wg_ref, ba_ref, bx_ref, clog_ref, s5g_ref, lrug_ref, wout_ref,
                  o_ref,
                  xtb, hbuf, ubuf, xrbuf, yrbuf, bubuf, stbuf, ybuf, s5state,
                  abuf, gbuf, hlru, lrustate, mixbuf):
    i = pl.program_id(0)
    f32 = jnp.float32
    bf16 = jnp.bfloat16

    @pl.when(i == 0)
    def _():
        s5state[...] = jnp.zeros_like(s5state)
        lrustate[...] = jnp.zeros_like(lrustate)
        xrbuf[pl.ds(0, HIST_ROWS), :] = jnp.zeros((HIST_ROWS, W_LRU), f32)

    x2d = x_ref.reshape(BATCH * CHUNK_SEG, LANES)
    o2d = o_ref.reshape(BATCH * CHUNK_SEG, LANES)
    g1 = g1_ref[...]

    def norm_body(t, c):
        xt = jnp.concatenate(
            [x2d[pl.ds(t * LANE_CHUNKS + j, BATCH, stride=CHUNK_SEG), :]
             for j in range(LANE_CHUNKS)], axis=-1)
        r0 = pl.multiple_of(t * BATCH, BATCH)
        xtb[pl.ds(r0, BATCH), :] = xt
        hbuf[pl.ds(r0, BATCH), :] = _rms(xt, g1).astype(bf16)
        return c
    lax.fori_loop(0, TIME_CHUNK, norm_body, 0)

    h = hbuf[...]
    ubuf[...] = jnp.dot(h, win_ref[:, 0:W_S5], preferred_element_type=f32)
    xrbuf[pl.ds(HIST_ROWS, MIX_ROWS), :] = jnp.dot(
        h, win_ref[:, W_S5:W_S5 + W_LRU], preferred_element_type=f32)
    yrbuf[...] = jnp.dot(h, win_ref[:, W_S5 + W_LRU:], preferred_element_type=f32)

    for hf in range(S5_HALVES):
        bubuf[...] = _bdot(ubuf[:, hf * MXU_TILE:(hf + 1) * MXU_TILE], wb_ref[hf])
        for q in range(HALF_STATES // SCAN_SLAB):
            re0 = q * SCAN_SLAB
            im0 = HALF_STATES + q * SCAN_SLAB
            sbase = hf * 2 * HALF_STATES
            lr = lam_ref[hf:hf + 1, re0:re0 + SCAN_SLAB]
            li = lam_ref[hf:hf + 1, im0:im0 + SCAN_SLAB]
            lr = jnp.broadcast_to(lr, (BATCH, SCAN_SLAB))
            li = jnp.broadcast_to(li, (BATCH, SCAN_SLAB))

            def scan_body(t, carry, re0=re0, im0=im0, lr=lr, li=li):
                sre, sim = carry
                r0 = pl.multiple_of(t * BATCH, BATCH)
                bre = bubuf[pl.ds(r0, BATCH), re0:re0 + SCAN_SLAB]
                bim = bubuf[pl.ds(r0, BATCH), im0:im0 + SCAN_SLAB]
                nre = lr * sre - li * sim + bre
                nim = lr * sim + li * sre + bim
                stbuf[pl.ds(r0, BATCH), re0:re0 + SCAN_SLAB] = nre.astype(bf16)
                stbuf[pl.ds(r0, BATCH), im0:im0 + SCAN_SLAB] = nim.astype(bf16)
                return nre, nim

            sre0 = s5state[:, sbase + re0:sbase + re0 + SCAN_SLAB]
            sim0 = s5state[:, sbase + im0:sbase + im0 + SCAN_SLAB]
            sre, sim = lax.fori_loop(0, TIME_CHUNK, scan_body, (sre0, sim0), unroll=True)
            s5state[:, sbase + re0:sbase + re0 + SCAN_SLAB] = sre
            s5state[:, sbase + im0:sbase + im0 + SCAN_SLAB] = sim
        ybuf[:, hf * MXU_TILE:(hf + 1) * MXU_TILE] = jnp.dot(
            stbuf[...], wc_ref[hf], preferred_element_type=f32)

    y = ybuf[...] + d_ref[...] * ubuf[...]
    z = jax.nn.gelu(y)
    z = z * jax.nn.sigmoid(_bdot(z, wglu_ref[...]) + bglu_ref[...])
    mixbuf[:, 0:W_S5] = _rms(z, s5g_ref[...]).astype(bf16)

    conv = cb_ref[...] + sum(
        xrbuf[pl.ds(k * BATCH, MIX_ROWS), :] * cw_ref[k:k + 1, :] for k in range(CONV_WIDTH))
    xrbuf[pl.ds(0, HIST_ROWS), :] = xrbuf[pl.ds(MIX_ROWS, HIST_ROWS), :]
    for hf in range(2):
        c0 = hf * MXU_TILE
        cv = conv[:, c0:c0 + MXU_TILE]
        ri = _bdot(cv, wg_ref[hf])
        r = jax.nn.sigmoid(ri[:, 0:MXU_TILE] + ba_ref[:, c0:c0 + MXU_TILE])
        ig = jax.nn.sigmoid(ri[:, MXU_TILE:] + bx_ref[:, c0:c0 + MXU_TILE])
        log_a = clog_ref[:, c0:c0 + MXU_TILE] * r
        a = jnp.exp(log_a)
        abuf[:, c0:c0 + MXU_TILE] = a
        one_minus_a2 = -jnp.tanh(log_a) * (a * a + 1.0)
        gbuf[:, c0:c0 + MXU_TILE] = jnp.sqrt(one_minus_a2) * (ig * cv)

    def lru_body(t, hstate):
        r0 = pl.multiple_of(t * BATCH, BATCH)
        hn = abuf[pl.ds(r0, BATCH), :] * hstate + gbuf[pl.ds(r0, BATCH), :]
        hlru[pl.ds(r0, BATCH), :] = hn
        return hn
    lrustate[...] = lax.fori_loop(0, TIME_CHUNK, lru_body, lrustate[...], unroll=True)

    lru_out = hlru[...] * jax.nn.gelu(yrbuf[...])
    mixbuf[:, W_S5:] = _rms(lru_out, lrug_ref[...]).astype(bf16)

    xtb[...] = xtb[...] + jnp.dot(mixbuf[...], wout_ref[...], preferred_element_type=f32)

    def out_body(t, c):
        r0 = pl.multiple_of(t * BATCH, BATCH)
        row = xtb[pl.ds(r0, BATCH), :]
        for j in range(LANE_CHUNKS):
            o2d[pl.ds(t * LANE_CHUNKS + j, BATCH, stride=CHUNK_SEG), :] = (
                row[:, j * LANES:(j + 1) * LANES])
        return c
    lax.fori_loop(0, TIME_CHUNK, out_body, 0)


def _ffn_kernel(x_ref, g2_ref, wgate_ref, wup_ref, wdown_ref, gf_ref, o_ref, actbuf):
    f32 = jnp.float32
    bf16 = jnp.bfloat16
    x = x_ref[...]
    h2 = _rms(x, g2_ref[...]).astype(bf16)
    for c in range(D_FF // FF_CHUNK):
        c0 = c * FF_CHUNK
        gate = jnp.dot(h2, wgate_ref[:, c0:c0 + FF_CHUNK], preferred_element_type=f32)
        up = jnp.dot(h2, wup_ref[:, c0:c0 + FF_CHUNK], preferred_element_type=f32)
        actbuf[:, c0:c0 + FF_CHUNK] = (jax.nn.silu(gate) * up).astype(bf16)
    x2 = x + jnp.dot(actbuf[...], wdown_ref[...], preferred_element_type=f32)
    o_ref[...] = _rms(x2, gf_ref[...])


def _const_spec(shape):
    nd = len(shape)
    return pl.BlockSpec(shape, lambda i, _nd=nd: (0,) * _nd, pipeline_mode=pl.Buffered(1))


def _block_diag(blocks):
    p, n, r, c = blocks.shape
    eye = jnp.eye(n, dtype=blocks.dtype)
    return jnp.einsum('pkrc,kj->pkrjc', blocks, eye).reshape(p, n * r, n * c)


def _prepare_s5(lam_re, lam_im, log_step, b_re, b_im, c_re, c_im):
    f32 = jnp.float32
    lam = lax.complex(jnp.minimum(lam_re.astype(f32), -1e-4), lam_im.astype(f32))
    step = jnp.exp(log_step.astype(f32))[:, None]
    lam_bar = jnp.exp(lam * step)
    b_bar = ((lam_bar - 1.0) / lam)[..., None] * lax.complex(b_re.astype(f32), b_im.astype(f32))

    def b_half(v):
        v = v.reshape(S5_HALVES, HALF_GROUPS, S5_STATE, S5_GROUP_CH).transpose(0, 1, 3, 2)
        return _block_diag(v)

    def c_half(v):
        v = v.reshape(S5_HALVES, HALF_GROUPS, S5_GROUP_CH, S5_STATE).transpose(0, 1, 3, 2)
        return _block_diag(v)

    wb = jnp.concatenate([b_half(jnp.real(b_bar)), b_half(jnp.imag(b_bar))], axis=2)
    wc = jnp.concatenate([c_half(c_re.astype(f32)), -c_half(c_im.astype(f32))], axis=1)
    lam_v = jnp.concatenate([jnp.real(lam_bar).reshape(S5_HALVES, HALF_STATES),
                             jnp.imag(lam_bar).reshape(S5_HALVES, HALF_STATES)], axis=1)
    return wb.astype(jnp.bfloat16), wc.astype(jnp.bfloat16), lam_v


def _prepare_lru_gates(w_a, w_x):
    def half(w):
        return _block_diag(w.astype(jnp.float32).reshape(2, LRU_HALF_HEADS, LRU_HEAD_DIM, LRU_HEAD_DIM))
    return jnp.concatenate([half(w_a), half(w_x)], axis=2).astype(jnp.bfloat16)


def _layer(x, norm1_g, w_in, s5_lambda_re, s5_lambda_im, s5_log_step, s5_b_re, s5_b_im,
           s5_c_re, s5_c_im, s5_d, s5_w_glu, s5_b_glu, lru_conv_w, lru_conv_b, lru_w_a, lru_b_a,
           lru_w_x, lru_b_x, lru_lambda, s5_out_g, lru_out_g, w_out, norm2_g, w_gate, w_up,
           w_down):
    f32 = jnp.float32
    bf16 = jnp.bfloat16
    wb, wc, lam_v = _prepare_s5(s5_lambda_re, s5_lambda_im, s5_log_step, s5_b_re, s5_b_im,
                                s5_c_re, s5_c_im)
    wg = _prepare_lru_gates(lru_w_a, lru_w_x)
    clog = (RG_C * jax.nn.log_sigmoid(lru_lambda.astype(f32))).reshape(1, W_LRU)
    row = lambda v, n: v.astype(f32).reshape(1, n)

    mixer_inputs = [
        x.reshape(BATCH, SEQ * LANE_CHUNKS, LANES),
        row(norm1_g, D_MODEL),
        w_in.astype(bf16),
        wb, lam_v, wc,
        row(s5_d, W_S5),
        s5_w_glu.astype(bf16),
        row(s5_b_glu, W_S5),
        lru_conv_w.astype(f32),
        row(lru_conv_b, W_LRU),
        wg,
        row(lru_b_a, W_LRU),
        row(lru_b_x, W_LRU),
        clog,
        row(s5_out_g, W_S5),
        row(lru_out_g, W_LRU),
        w_out.astype(bf16),
    ]
    x_spec = pl.BlockSpec((BATCH, CHUNK_SEG, LANES), lambda i: (0, i, 0))
    in_specs = [x_spec] + [_const_spec(a.shape) for a in mixer_inputs[1:]]
    scratch = [
        pltpu.VMEM((MIX_ROWS, D_MODEL), f32),
        pltpu.VMEM((MIX_ROWS, D_MODEL), bf16),
        pltpu.VMEM((MIX_ROWS, W_S5), f32),
        pltpu.VMEM((MIX_ROWS + HIST_ROWS, W_LRU), f32),
        pltpu.VMEM((MIX_ROWS, W_LRU), f32),
        pltpu.VMEM((MIX_ROWS, 2 * HALF_STATES), f32),
        pltpu.VMEM((MIX_ROWS, 2 * HALF_STATES), bf16),
        pltpu.VMEM((MIX_ROWS, W_S5), f32),
        pltpu.VMEM((BATCH, 2 * S5_GROUPS * S5_STATE), f32),
        pltpu.VMEM((MIX_ROWS, W_LRU), f32),
        pltpu.VMEM((MIX_ROWS, W_LRU), f32),
        pltpu.VMEM((MIX_ROWS, W_LRU), f32),
        pltpu.VMEM((BATCH, W_LRU), f32),
        pltpu.VMEM((MIX_ROWS, D_MODEL), bf16),
    ]
    x1 = pl.pallas_call(
        _mixer_kernel,
        grid=(SEQ // TIME_CHUNK,),
        in_specs=in_specs,
        out_specs=x_spec,
        out_shape=jax.ShapeDtypeStruct((BATCH, SEQ * LANE_CHUNKS, LANES), f32),
        scratch_shapes=scratch,
        compiler_params=pltpu.CompilerParams(
            dimension_semantics=("arbitrary",), vmem_limit_bytes=VMEM_LIMIT_BYTES),
        name="mixer",
    )(*mixer_inputs)
    return x1.reshape(BATCH * SEQ, D_MODEL), norm2_g, w_gate, w_up, w_down


def _ffn(x1, norm2_g, w_gate, w_up, w_down, final_g):
    f32 = jnp.float32
    bf16 = jnp.bfloat16
    rows = x1.shape[0]
    row_spec = pl.BlockSpec((FFN_ROWS, D_MODEL), lambda i: (i, 0))
    inputs = [x1, norm2_g.astype(f32).reshape(1, D_MODEL), w_gate.astype(bf16), w_up.astype(bf16),
              w_down.astype(bf16), final_g.astype(f32).reshape(1, D_MODEL)]
    return pl.pallas_call(
        _ffn_kernel,
        grid=(rows // FFN_ROWS,),
        in_specs=[row_spec] + [_const_spec(a.shape) for a in inputs[1:]],
        out_specs=row_spec,
        out_shape=jax.ShapeDtypeStruct((rows, D_MODEL), f32),
        scratch_shapes=[pltpu.VMEM((FFN_ROWS, D_FF), bf16)],
        compiler_params=pltpu.CompilerParams(
            dimension_semantics=("arbitrary",), vmem_limit_bytes=VMEM_LIMIT_BYTES),
        name="ffn",
    )(*inputs)


def kernel(x, norm1_g, w_in, s5_lambda_re, s5_lambda_im, s5_log_step, s5_b_re, s5_b_im, s5_c_re, s5_c_im, s5_d, s5_w_glu, s5_b_glu, lru_conv_w, lru_conv_b, lru_w_a, lru_b_a, lru_w_x, lru_b_x, lru_lambda, s5_out_g, lru_out_g, w_out, norm2_g, w_gate, w_up, w_down, final_g):
    x1, g2, wg, wu, wd = _layer(
        x, norm1_g[0], w_in[0], s5_lambda_re[0], s5_lambda_im[0], s5_log_step[0], s5_b_re[0],
        s5_b_im[0], s5_c_re[0], s5_c_im[0], s5_d[0], s5_w_glu[0], s5_b_glu[0], lru_conv_w[0],
        lru_conv_b[0], lru_w_a[0], lru_b_a[0], lru_w_x[0], lru_b_x[0], lru_lambda[0],
        s5_out_g[0], lru_out_g[0], w_out[0], norm2_g[0], w_gate[0], w_up[0], w_down[0])
    out = _ffn(x1, g2, wg, wu, wd, final_g)
    return out.reshape(BATCH, SEQ, D_MODEL)
```

```python
import functools
import math

import jax
import jax.numpy as jnp
from jax import lax
from jax.experimental import pallas as pl
from jax.experimental.pallas import tpu as pltpu

D_MODEL = 1024
BATCH = 16
SEQ = 2048
W_S5 = 512
S5_GROUP_CH = 16
S5_GROUPS = 32
S5_STATE = 64
W_LRU = 512
LRU_HEADS = 8
LRU_HEAD_DIM = 64
CONV_WIDTH = 4
RG_C = 8.0
D_FF = 2816
EPS = 1e-6

MXU_TILE = 256
S5_HALVES = W_S5 // MXU_TILE
HALF_GROUPS = S5_GROUPS // S5_HALVES
HALF_STATES = HALF_GROUPS * S5_STATE
SCAN_SLAB = 512
LRU_HALF_HEADS = LRU_HEADS // 2

TIME_CHUNK = 16
MIX_ROWS = TIME_CHUNK * BATCH
HIST_ROWS = (CONV_WIDTH - 1) * BATCH
LANES = 128
LANE_CHUNKS = D_MODEL // LANES
FFN_ROWS = 512
FF_CHUNK = 256
VMEM_LIMIT_BYTES = 56 * 1024 * 1024


def _rms(x, g):
    ms = jnp.mean(x * x, axis=-1, keepdims=True)
    return x * lax.rsqrt(ms + EPS) * g


def _bdot(a, b):
    return jnp.dot(a.astype(jnp.bfloat16), b, preferred_element_type=jnp.float32)


def _mixer_kernel(x_ref, g1_ref, win_ref, wb_ref, lam_ref, wc_ref, d_ref, wglu_ref, bglu_ref,
                  cw_ref, cb_ref, wg_ref, ba_ref, bx_ref, clog_ref, s5g_ref, lrug_ref, wout_ref,
                  o_ref,
                  xs, osb, xtb, hbuf, ubuf, xrbuf, yrbuf, bubuf, stbuf, ybuf, s5state,
                  abuf, gbuf, hlru, lrustate, mixbuf):
    i = pl.program_id(0)
    f32 = jnp.float32
    bf16 = jnp.bfloat16

    @pl.when(i == 0)
    def _():
        s5state[...] = jnp.zeros_like(s5state)
        lrustate[...] = jnp.zeros_like(lrustate)
        xrbuf[pl.ds(0, HIST_ROWS), :] = jnp.zeros((HIST_ROWS, W_LRU), f32)

    x2d = x_ref.reshape(MIX_ROWS, D_MODEL)
    o2d = o_ref.reshape(MIX_ROWS, D_MODEL)
    g1 = g1_ref[...]
    for j in range(LANE_CHUNKS):
        xs[j] = x2d[:, j * LANES:(j + 1) * LANES]

    def norm_body(t, c):
        xt = jnp.concatenate(
            [xs[j, pl.ds(t, BATCH, stride=TIME_CHUNK), :] for j in range(LANE_CHUNKS)], axis=-1)
        r0 = pl.multiple_of(t * BATCH, BATCH)
        xtb[pl.ds(r0, BATCH), :] = xt
        hbuf[pl.ds(r0, BATCH), :] = _rms(xt, g1).astype(bf16)
        return c
    lax.fori_loop(0, TIME_CHUNK, norm_body, 0)

    h = hbuf[...]
    ubuf[...] = jnp.dot(h, win_ref[:, 0:W_S5], preferred_element_type=f32)
    xrbuf[pl.ds(HIST_ROWS, MIX_ROWS), :] = jnp.dot(
        h, win_ref[:, W_S5:W_S5 + W_LRU], preferred_element_type=f32)
    yrbuf[...] = jnp.dot(h, win_ref[:, W_S5 + W_LRU:], preferred_element_type=f32)

    for hf in range(S5_HALVES):
        bubuf[...] = _bdot(ubuf[:, hf * MXU_TILE:(hf + 1) * MXU_TILE], wb_ref[hf])
        for q in range(HALF_STATES // SCAN_SLAB):
            re0 = q * SCAN_SLAB
            im0 = HALF_STATES + q * SCAN_SLAB
            sbase = hf * 2 * HALF_STATES
            lr = lam_ref[hf:hf + 1, re0:re0 + SCAN_SLAB]
            li = lam_ref[hf:hf + 1, im0:im0 + SCAN_SLAB]
            lr = jnp.broadcast_to(lr, (BATCH, SCAN_SLAB))
            li = jnp.broadcast_to(li, (BATCH, SCAN_SLAB))

            def scan_body(t, carry, re0=re0, im0=im0, lr=lr, li=li):
                sre, sim = carry
                r0 = pl.multiple_of(t * BATCH, BATCH)
                bre = bubuf[pl.ds(r0, BATCH), re0:re0 + SCAN_SLAB]
                bim = bubuf[pl.ds(r0, BATCH), im0:im0 + SCAN_SLAB]
                nre = lr * sre - li * sim + bre
                nim = lr * sim + li * sre + bim
                stbuf[pl.ds(r0, BATCH), re0:re0 + SCAN_SLAB] = nre.astype(bf16)
                stbuf[pl.ds(r0, BATCH), im0:im0 + SCAN_SLAB] = nim.astype(bf16)
                return nre, nim

            sre0 = s5state[:, sbase + re0:sbase + re0 + SCAN_SLAB]
            sim0 = s5state[:, sbase + im0:sbase + im0 + SCAN_SLAB]
            sre, sim = lax.fori_loop(0, TIME_CHUNK, scan_body, (sre0, sim0), unroll=True)
            s5state[:, sbase + re0:sbase + re0 + SCAN_SLAB] = sre
            s5state[:, sbase + im0:sbase + im0 + SCAN_SLAB] = sim
        ybuf[:, hf * MXU_TILE:(hf + 1) * MXU_TILE] = jnp.dot(
            stbuf[...], wc_ref[hf], preferred_element_type=f32)

    y = ybuf[...] + d_ref[...] * ubuf[...]
    z = jax.nn.gelu(y)
    z = z * jax.nn.sigmoid(_bdot(z, wglu_ref[...]) + bglu_ref[...])
    mixbuf[:, 0:W_S5] = _rms(z, s5g_ref[...]).astype(bf16)

    conv = cb_ref[...] + sum(
        xrbuf[pl.ds(k * BATCH, MIX_ROWS), :] * cw_ref[k:k + 1, :] for k in range(CONV_WIDTH))
    xrbuf[pl.ds(0, HIST_ROWS), :] = xrbuf[pl.ds(MIX_ROWS, HIST_ROWS), :]
    for hf in range(2):
        c0 = hf * MXU_TILE
        cv = conv[:, c0:c0 + MXU_TILE]
        ri = _bdot(cv, wg_ref[hf])
        r = jax.nn.sigmoid(ri[:, 0:MXU_TILE] + ba_ref[:, c0:c0 + MXU_TILE])
        ig = jax.nn.sigmoid(ri[:, MXU_TILE:] + bx_ref[:, c0:c0 + MXU_TILE])
        log_a = clog_ref[:, c0:c0 + MXU_TILE] * r
        a = jnp.exp(log_a)
        abuf[:, c0:c0 + MXU_TILE] = a
        one_minus_a2 = -jnp.tanh(log_a) * (a * a + 1.0)
        gbuf[:, c0:c0 + MXU_TILE] = jnp.sqrt(one_minus_a2) * (ig * cv)

    def lru_body(t, hstate):
        r0 = pl.multiple_of(t * BATCH, BATCH)
        hn = abuf[pl.ds(r0, BATCH), :] * hstate + gbuf[pl.ds(r0, BATCH), :]
        hlru[pl.ds(r0, BATCH), :] = hn
        return hn
    lrustate[...] = lax.fori_loop(0, TIME_CHUNK, lru_body, lrustate[...], unroll=True)

    lru_out = hlru[...] * jax.nn.gelu(yrbuf[...])
    mixbuf[:, W_S5:] = _rms(lru_out, lrug_ref[...]).astype(bf16)

    xtb[...] = xtb[...] + jnp.dot(mixbuf[...], wout_ref[...], preferred_element_type=f32)

    def out_body(t, c):
        r0 = pl.multiple_of(t * BATCH, BATCH)
        row = xtb[pl.ds(r0, BATCH), :]
        for j in range(LANE_CHUNKS):
            osb[j, pl.ds(t, BATCH, stride=TIME_CHUNK), :] = row[:, j * LANES:(j + 1) * LANES]
        return c
    lax.fori_loop(0, TIME_CHUNK, out_body, 0)
    for j in range(LANE_CHUNKS):
        o2d[:, j * LANES:(j + 1) * LANES] = osb[j]


def _ffn_kernel(x_ref, g2_ref, wgate_ref, wup_ref, wdown_ref, gf_ref, o_ref, actbuf):
    f32 = jnp.float32
    bf16 = jnp.bfloat16
    x = x_ref[...]
    h2 = _rms(x, g2_ref[...]).astype(bf16)
    for c in range(D_FF // FF_CHUNK):
        c0 = c * FF_CHUNK
        gate = jnp.dot(h2, wgate_ref[:, c0:c0 + FF_CHUNK], preferred_element_type=f32)
        up = jnp.dot(h2, wup_ref[:, c0:c0 + FF_CHUNK], preferred_element_type=f32)
        actbuf[:, c0:c0 + FF_CHUNK] = (jax.nn.silu(gate) * up).astype(bf16)
    x2 = x + jnp.dot(actbuf[...], wdown_ref[...], preferred_element_type=f32)
    o_ref[...] = _rms(x2, gf_ref[...])


def _const_spec(shape):
    nd = len(shape)
    return pl.BlockSpec(shape, lambda i, _nd=nd: (0,) * _nd, pipeline_mode=pl.Buffered(1))


def _block_diag(blocks):
    p, n, r, c = blocks.shape
    eye = jnp.eye(n, dtype=blocks.dtype)
    return jnp.einsum('pkrc,kj->pkrjc', blocks, eye).reshape(p, n * r, n * c)


def _prepare_s5(lam_re, lam_im, log_step, b_re, b_im, c_re, c_im):
    f32 = jnp.float32
    la = jnp.minimum(lam_re.astype(f32), -1e-4)
    lb = lam_im.astype(f32)
    step = jnp.exp(log_step.astype(f32))[:, None]
    mag = jnp.exp(la * step)
    lbar_re = mag * jnp.cos(lb * step)
    lbar_im = mag * jnp.sin(lb * step)
    den = la * la + lb * lb
    q_re = (((lbar_re - 1.0) * la + lbar_im * lb) / den)[..., None]
    q_im = ((lbar_im * la - (lbar_re - 1.0) * lb) / den)[..., None]
    bre = b_re.astype(f32)
    bim = b_im.astype(f32)
    bbar_re = q_re * bre - q_im * bim
    bbar_im = q_re * bim + q_im * bre

    def b_half(v):
        v = v.reshape(S5_HALVES, HALF_GROUPS, S5_STATE, S5_GROUP_CH).transpose(0, 1, 3, 2)
        return _block_diag(v)

    def c_half(v):
        v = v.reshape(S5_HALVES, HALF_GROUPS, S5_GROUP_CH, S5_STATE).transpose(0, 1, 3, 2)
        return _block_diag(v)

    wb = jnp.concatenate([b_half(bbar_re), b_half(bbar_im)], axis=2)
    wc = jnp.concatenate([c_half(c_re.astype(f32)), -c_half(c_im.astype(f32))], axis=1)
    lam_v = jnp.concatenate([lbar_re.reshape(S5_HALVES, HALF_STATES),
                             lbar_im.reshape(S5_HALVES, HALF_STATES)], axis=1)
    return wb.astype(jnp.bfloat16), wc.astype(jnp.bfloat16), lam_v


def _prepare_lru_gates(w_a, w_x):
    def half(w):
        return _block_diag(w.astype(jnp.float32).reshape(2, LRU_HALF_HEADS, LRU_HEAD_DIM, LRU_HEAD_DIM))
    return jnp.concatenate([half(w_a), half(w_x)], axis=2).astype(jnp.bfloat16)


def _layer(x, norm1_g, w_in, s5_lambda_re, s5_lambda_im, s5_log_step, s5_b_re, s5_b_im,
           s5_c_re, s5_c_im, s5_d, s5_w_glu, s5_b_glu, lru_conv_w, lru_conv_b, lru_w_a, lru_b_a,
           lru_w_x, lru_b_x, lru_lambda, s5_out_g, lru_out_g, w_out, norm2_g, w_gate, w_up,
           w_down):
    f32 = jnp.float32
    bf16 = jnp.bfloat16
    wb, wc, lam_v = _prepare_s5(s5_lambda_re, s5_lambda_im, s5_log_step, s5_b_re, s5_b_im,
                                s5_c_re, s5_c_im)
    wg = _prepare_lru_gates(lru_w_a, lru_w_x)
    clog = (RG_C * jax.nn.log_sigmoid(lru_lambda.astype(f32))).reshape(1, W_LRU)
    row = lambda v, n: v.astype(f32).reshape(1, n)

    mixer_inputs = [
        x,
        row(norm1_g, D_MODEL),
        w_in.astype(bf16),
        wb, lam_v, wc,
        row(s5_d, W_S5),
        s5_w_glu.astype(bf16),
        row(s5_b_glu, W_S5),
        lru_conv_w.astype(f32),
        row(lru_conv_b, W_LRU),
        wg,
        row(lru_b_a, W_LRU),
        row(lru_b_x, W_LRU),
        clog,
        row(s5_out_g, W_S5),
        row(lru_out_g, W_LRU),
        w_out.astype(bf16),
    ]
    x_spec = pl.BlockSpec((BATCH, TIME_CHUNK, D_MODEL), lambda i: (0, i, 0))
    in_specs = [x_spec] + [_const_spec(a.shape) for a in mixer_inputs[1:]]
    scratch = [
        pltpu.VMEM((LANE_CHUNKS, MIX_ROWS, LANES), f32),
        pltpu.VMEM((LANE_CHUNKS, MIX_ROWS, LANES), f32),
        pltpu.VMEM((MIX_ROWS, D_MODEL), f32),
        pltpu.VMEM((MIX_ROWS, D_MODEL), bf16),
        pltpu.VMEM((MIX_ROWS, W_S5), f32),
        pltpu.VMEM((MIX_ROWS + HIST_ROWS, W_LRU), f32),
        pltpu.VMEM((MIX_ROWS, W_LRU), f32),
        pltpu.VMEM((MIX_ROWS, 2 * HALF_STATES), f32),
        pltpu.VMEM((MIX_ROWS, 2 * HALF_STATES), bf16),
        pltpu.VMEM((MIX_ROWS, W_S5), f32),
        pltpu.VMEM((BATCH, 2 * S5_GROUPS * S5_STATE), f32),
        pltpu.VMEM((MIX_ROWS, W_LRU), f32),
        pltpu.VMEM((MIX_ROWS, W_LRU), f32),
        pltpu.VMEM((MIX_ROWS, W_LRU), f32),
        pltpu.VMEM((BATCH, W_LRU), f32),
        pltpu.VMEM((MIX_ROWS, D_MODEL), bf16),
    ]
    x1 = pl.pallas_call(
        _mixer_kernel,
        grid=(SEQ // TIME_CHUNK,),
        in_specs=in_specs,
        out_specs=x_spec,
        out_shape=jax.ShapeDtypeStruct((BATCH, SEQ, D_MODEL), f32),
        scratch_shapes=scratch,
        compiler_params=pltpu.CompilerParams(
            dimension_semantics=("arbitrary",), vmem_limit_bytes=VMEM_LIMIT_BYTES),
        name="mixer",
    )(*mixer_inputs)
    return x1.reshape(BATCH * SEQ, D_MODEL), norm2_g, w_gate, w_up, w_down


def _ffn(x1, norm2_g, w_gate, w_up, w_down, final_g):
    f32 = jnp.float32
    bf16 = jnp.bfloat16
    rows = x1.shape[0]
    row_spec = pl.BlockSpec((FFN_ROWS, D_MODEL), lambda i: (i, 0))
    inputs = [x1, norm2_g.astype(f32).reshape(1, D_MODEL), w_gate.astype(bf16), w_up.astype(bf16),
              w_down.astype(bf16), final_g.astype(f32).reshape(1, D_MODEL)]
    return pl.pallas_call(
        _ffn_kernel,
        grid=(rows // FFN_ROWS,),
        in_specs=[row_spec] + [_const_spec(a.shape) for a in inputs[1:]],
        out_specs=row_spec,
        out_shape=jax.ShapeDtypeStruct((rows, D_MODEL), f32),
        scratch_shapes=[pltpu.VMEM((FFN_ROWS, D_FF), bf16)],
        compiler_params=pltpu.CompilerParams(
            dimension_semantics=("arbitrary",), vmem_limit_bytes=VMEM_LIMIT_BYTES),
        name="ffn",
    )(*inputs)


def kernel(x, norm1_g, w_in, s5_lambda_re, s5_lambda_im, s5_log_step, s5_b_re, s5_b_im, s5_c_re, s5_c_im, s5_d, s5_w_glu, s5_b_glu, lru_conv_w, lru_conv_b, lru_w_a, lru_b_a, lru_w_x, lru_b_x, lru_lambda, s5_out_g, lru_out_g, w_out, norm2_g, w_gate, w_up, w_down, final_g):
    x1, g2, wg, wu, wd = _layer(
        x, norm1_g[0], w_in[0], s5_lambda_re[0], s5_lambda_im[0], s5_log_step[0], s5_b_re[0],
        s5_b_im[0], s5_c_re[0], s5_c_im[0], s5_d[0], s5_w_glu[0], s5_b_glu[0], lru_conv_w[0],
        lru_conv_b[0], lru_w_a[0], lru_b_a[0], lru_w_x[0], lru_b_x[0], lru_lambda[0],
        s5_out_g[0], lru_out_g[0], w_out[0], norm2_g[0], w_gate[0], w_up[0], w_down[0])
    out = _ffn(x1, g2, wg, wu, wd, final_g)
    return out.reshape(BATCH, SEQ, D_MODEL)
```

```python
import functools
import math

import jax
import jax.numpy as jnp
from jax import lax
from jax.experimental import pallas as pl
from jax.experimental.pallas import tpu as pltpu

D_MODEL = 1024
BATCH = 16
SEQ = 2048
W_S5 = 512
S5_GROUP_CH = 16
S5_GROUPS = 32
S5_STATE = 64
W_LRU = 512
LRU_HEADS = 8
LRU_HEAD_DIM = 64
CONV_WIDTH = 4
RG_C = 8.0
D_FF = 2816
EPS = 1e-6

MXU_TILE = 256
S5_HALVES = W_S5 // MXU_TILE
HALF_GROUPS = S5_GROUPS // S5_HALVES
HALF_STATES = HALF_GROUPS * S5_STATE
SCAN_SLAB = 512
LRU_HALF_HEADS = LRU_HEADS // 2

TIME_CHUNK = 32
MIX_ROWS = TIME_CHUNK * BATCH
HIST_ROWS = (CONV_WIDTH - 1) * BATCH
LANES = 128
LANE_CHUNKS = D_MODEL // LANES
SUBLANES = 8
STAGE_SEG = TIME_CHUNK + SUBLANES
STAGE_ROWS = BATCH * STAGE_SEG
FFN_ROWS = 512
FF_CHUNK = 256
VMEM_LIMIT_BYTES = 56 * 1024 * 1024


def _rms(x, g):
    ms = jnp.mean(x * x, axis=-1, keepdims=True)
    return x * lax.rsqrt(ms + EPS) * g


def _bdot(a, b):
    return jnp.dot(a.astype(jnp.bfloat16), b, preferred_element_type=jnp.float32)


def _mixer_kernel(x_ref, g1_ref, win_ref, wb_ref, lam_ref, wc_ref, d_ref, wglu_ref, bglu_ref,
                  cw_ref, cb_ref, wg_ref, ba_ref, bx_ref, clog_ref, s5g_ref, lrug_ref, wout_ref,
                  o_ref,
                  xs, osb, xtb, hbuf, ubuf, xrbuf, yrbuf, bubuf, stbuf, ybuf, s5state,
                  abuf, gbuf, hlru, lrustate, mixbuf):
    i = pl.program_id(0)
    f32 = jnp.float32
    bf16 = jnp.bfloat16

    @pl.when(i == 0)
    def _():
        s5state[...] = jnp.zeros_like(s5state)
        lrustate[...] = jnp.zeros_like(lrustate)
        xrbuf[pl.ds(0, HIST_ROWS), :] = jnp.zeros((HIST_ROWS, W_LRU), f32)

    x2d = x_ref.reshape(MIX_ROWS, D_MODEL)
    o2d = o_ref.reshape(MIX_ROWS, D_MODEL)
    g1 = g1_ref[...]
    for b in range(BATCH):
        hn = _rms(x2d[b * TIME_CHUNK:(b + 1) * TIME_CHUNK, :], g1)
        for j in range(LANE_CHUNKS):
            xs[j, b * STAGE_SEG:b * STAGE_SEG + TIME_CHUNK, :] = hn[:, j * LANES:(j + 1) * LANES]
    for t in range(TIME_CHUNK):
        ht = jnp.concatenate(
            [xs[j, pl.ds(t, BATCH, stride=STAGE_SEG), :] for j in range(LANE_CHUNKS)], axis=-1)
        hbuf[t * BATCH:(t + 1) * BATCH, :] = ht.astype(bf16)

    h = hbuf[...]
    ubuf[...] = jnp.dot(h, win_ref[:, 0:W_S5], preferred_element_type=f32)
    xrbuf[pl.ds(HIST_ROWS, MIX_ROWS), :] = jnp.dot(
        h, win_ref[:, W_S5:W_S5 + W_LRU], preferred_element_type=f32)
    yrbuf[...] = jnp.dot(h, win_ref[:, W_S5 + W_LRU:], preferred_element_type=f32)

    for hf in range(S5_HALVES):
        bubuf[...] = _bdot(ubuf[:, hf * MXU_TILE:(hf + 1) * MXU_TILE], wb_ref[hf])
        for q in range(HALF_STATES // SCAN_SLAB):
            re0 = q * SCAN_SLAB
            im0 = HALF_STATES + q * SCAN_SLAB
            sbase = hf * 2 * HALF_STATES
            lr = lam_ref[hf:hf + 1, re0:re0 + SCAN_SLAB]
            li = lam_ref[hf:hf + 1, im0:im0 + SCAN_SLAB]
            lr = jnp.broadcast_to(lr, (BATCH, SCAN_SLAB))
            li = jnp.broadcast_to(li, (BATCH, SCAN_SLAB))

            def scan_body(t, carry, re0=re0, im0=im0, lr=lr, li=li):
                sre, sim = carry
                r0 = pl.multiple_of(t * BATCH, BATCH)
                bre = bubuf[pl.ds(r0, BATCH), re0:re0 + SCAN_SLAB]
                bim = bubuf[pl.ds(r0, BATCH), im0:im0 + SCAN_SLAB]
                nre = lr * sre - li * sim + bre
                nim = lr * sim + li * sre + bim
                stbuf[pl.ds(r0, BATCH), re0:re0 + SCAN_SLAB] = nre.astype(bf16)
                stbuf[pl.ds(r0, BATCH), im0:im0 + SCAN_SLAB] = nim.astype(bf16)
                return nre, nim

            sre0 = s5state[:, sbase + re0:sbase + re0 + SCAN_SLAB]
            sim0 = s5state[:, sbase + im0:sbase + im0 + SCAN_SLAB]
            sre, sim = lax.fori_loop(0, TIME_CHUNK, scan_body, (sre0, sim0), unroll=True)
            s5state[:, sbase + re0:sbase + re0 + SCAN_SLAB] = sre
            s5state[:, sbase + im0:sbase + im0 + SCAN_SLAB] = sim
        ybuf[:, hf * MXU_TILE:(hf + 1) * MXU_TILE] = jnp.dot(
            stbuf[...], wc_ref[hf], preferred_element_type=f32)

    y = ybuf[...] + d_ref[...] * ubuf[...]
    z = jax.nn.gelu(y)
    z = z * jax.nn.sigmoid(_bdot(z, wglu_ref[...]) + bglu_ref[...])
    mixbuf[:, 0:W_S5] = _rms(z, s5g_ref[...]).astype(bf16)

    conv = cb_ref[...] + sum(
        xrbuf[pl.ds(k * BATCH, MIX_ROWS), :] * cw_ref[k:k + 1, :] for k in range(CONV_WIDTH))
    xrbuf[pl.ds(0, HIST_ROWS), :] = xrbuf[pl.ds(MIX_ROWS, HIST_ROWS), :]
    for hf in range(2):
        c0 = hf * MXU_TILE
        cv = conv[:, c0:c0 + MXU_TILE]
        ri = _bdot(cv, wg_ref[hf])
        r = jax.nn.sigmoid(ri[:, 0:MXU_TILE] + ba_ref[:, c0:c0 + MXU_TILE])
        ig = jax.nn.sigmoid(ri[:, MXU_TILE:] + bx_ref[:, c0:c0 + MXU_TILE])
        log_a = clog_ref[:, c0:c0 + MXU_TILE] * r
        a = jnp.exp(log_a)
        abuf[:, c0:c0 + MXU_TILE] = a
        one_minus_a2 = -jnp.tanh(log_a) * (a * a + 1.0)
        gbuf[:, c0:c0 + MXU_TILE] = jnp.sqrt(one_minus_a2) * (ig * cv)

    def lru_body(t, hstate):
        r0 = pl.multiple_of(t * BATCH, BATCH)
        hn = abuf[pl.ds(r0, BATCH), :] * hstate + gbuf[pl.ds(r0, BATCH), :]
        hlru[pl.ds(r0, BATCH), :] = hn
        return hn
    lrustate[...] = lax.fori_loop(0, TIME_CHUNK, lru_body, lrustate[...], unroll=True)

    lru_out = hlru[...] * jax.nn.gelu(yrbuf[...])
    mixbuf[:, W_S5:] = _rms(lru_out, lrug_ref[...]).astype(bf16)

    xtb[...] = jnp.dot(mixbuf[...], wout_ref[...], preferred_element_type=f32)
    for t in range(TIME_CHUNK):
        row = xtb[t * BATCH:(t + 1) * BATCH, :]
        for j in range(LANE_CHUNKS):
            osb[j, pl.ds(t, BATCH, stride=STAGE_SEG), :] = row[:, j * LANES:(j + 1) * LANES]
    for b in range(BATCH):
        delta = jnp.concatenate(
            [osb[j, b * STAGE_SEG:b * STAGE_SEG + TIME_CHUNK, :] for j in range(LANE_CHUNKS)],
            axis=-1)
        rows = slice(b * TIME_CHUNK, (b + 1) * TIME_CHUNK)
        o2d[rows, :] = x2d[rows, :] + delta


def _ffn_kernel(x_ref, g2_ref, wgate_ref, wup_ref, wdown_ref, gf_ref, o_ref, actbuf):
    f32 = jnp.float32
    bf16 = jnp.bfloat16
    x = x_ref[...]
    h2 = _rms(x, g2_ref[...]).astype(bf16)
    for c in range(D_FF // FF_CHUNK):
        c0 = c * FF_CHUNK
        gate = jnp.dot(h2, wgate_ref[:, c0:c0 + FF_CHUNK], preferred_element_type=f32)
        up = jnp.dot(h2, wup_ref[:, c0:c0 + FF_CHUNK], preferred_element_type=f32)
        actbuf[:, c0:c0 + FF_CHUNK] = (jax.nn.silu(gate) * up).astype(bf16)
    x2 = x + jnp.dot(actbuf[...], wdown_ref[...], preferred_element_type=f32)
    o_ref[...] = _rms(x2, gf_ref[...])


def _const_spec(shape):
    nd = len(shape)
    return pl.BlockSpec(shape, lambda i, _nd=nd: (0,) * _nd, pipeline_mode=pl.Buffered(1))


def _block_diag(blocks):
    p, n, r, c = blocks.shape
    eye = jnp.eye(n, dtype=blocks.dtype)
    return jnp.einsum('pkrc,kj->pkrjc', blocks, eye).reshape(p, n * r, n * c)


def _prepare_s5(lam_re, lam_im, log_step, b_re, b_im, c_re, c_im):
    f32 = jnp.float32
    la = jnp.minimum(lam_re.astype(f32), -1e-4)
    lb = lam_im.astype(f32)
    step = jnp.exp(log_step.astype(f32))[:, None]
    mag = jnp.exp(la * step)
    lbar_re = mag * jnp.cos(lb * step)
    lbar_im = mag * jnp.sin(lb * step)
    den = la * la + lb * lb
    q_re = (((lbar_re - 1.0) * la + lbar_im * lb) / den)[..., None]
    q_im = ((lbar_im * la - (lbar_re - 1.0) * lb) / den)[..., None]
    bre = b_re.astype(f32)
    bim = b_im.astype(f32)
    bbar_re = q_re * bre - q_im * bim
    bbar_im = q_re * bim + q_im * bre

    def b_half(v):
        v = v.reshape(S5_HALVES, HALF_GROUPS, S5_STATE, S5_GROUP_CH).transpose(0, 1, 3, 2)
        return _block_diag(v)

    def c_half(v):
        v = v.reshape(S5_HALVES, HALF_GROUPS, S5_GROUP_CH, S5_STATE).transpose(0, 1, 3, 2)
        return _block_diag(v)

    wb = jnp.concatenate([b_half(bbar_re), b_half(bbar_im)], axis=2)
    wc = jnp.concatenate([c_half(c_re.astype(f32)), -c_half(c_im.astype(f32))], axis=1)
    lam_v = jnp.concatenate([lbar_re.reshape(S5_HALVES, HALF_STATES),
                             lbar_im.reshape(S5_HALVES, HALF_STATES)], axis=1)
    return wb.astype(jnp.bfloat16), wc.astype(jnp.bfloat16), lam_v


def _prepare_lru_gates(w_a, w_x):
    def half(w):
        return _block_diag(w.astype(jnp.float32).reshape(2, LRU_HALF_HEADS, LRU_HEAD_DIM, LRU_HEAD_DIM))
    return jnp.concatenate([half(w_a), half(w_x)], axis=2).astype(jnp.bfloat16)


def _layer(x, norm1_g, w_in, s5_lambda_re, s5_lambda_im, s5_log_step, s5_b_re, s5_b_im,
           s5_c_re, s5_c_im, s5_d, s5_w_glu, s5_b_glu, lru_conv_w, lru_conv_b, lru_w_a, lru_b_a,
           lru_w_x, lru_b_x, lru_lambda, s5_out_g, lru_out_g, w_out, norm2_g, w_gate, w_up,
           w_down):
    f32 = jnp.float32
    bf16 = jnp.bfloat16
    wb, wc, lam_v = _prepare_s5(s5_lambda_re, s5_lambda_im, s5_log_step, s5_b_re, s5_b_im,
                                s5_c_re, s5_c_im)
    wg = _prepare_lru_gates(lru_w_a, lru_w_x)
    clog = (RG_C * jax.nn.log_sigmoid(lru_lambda.astype(f32))).reshape(1, W_LRU)
    row = lambda v, n: v.astype(f32).reshape(1, n)

    mixer_inputs = [
        x,
        row(norm1_g, D_MODEL),
        w_in.astype(bf16),
        wb, lam_v, wc,
        row(s5_d, W_S5),
        s5_w_glu.astype(bf16),
        row(s5_b_glu, W_S5),
        lru_conv_w.astype(f32),
        row(lru_conv_b, W_LRU),
        wg,
        row(lru_b_a, W_LRU),
        row(lru_b_x, W_LRU),
        clog,
        row(s5_out_g, W_S5),
        row(lru_out_g, W_LRU),
        w_out.astype(bf16),
    ]
    x_spec = pl.BlockSpec((BATCH, TIME_CHUNK, D_MODEL), lambda i: (0, i, 0))
    in_specs = [x_spec] + [_const_spec(a.shape) for a in mixer_inputs[1:]]
    scratch = [
        pltpu.VMEM((LANE_CHUNKS, STAGE_ROWS, LANES), f32),
        pltpu.VMEM((LANE_CHUNKS, STAGE_ROWS, LANES), f32),
        pltpu.VMEM((MIX_ROWS, D_MODEL), f32),
        pltpu.VMEM((MIX_ROWS, D_MODEL), bf16),
        pltpu.VMEM((MIX_ROWS, W_S5), f32),
        pltpu.VMEM((MIX_ROWS + HIST_ROWS, W_LRU), f32),
        pltpu.VMEM((MIX_ROWS, W_LRU), f32),
        pltpu.VMEM((MIX_ROWS, 2 * HALF_STATES), f32),
        pltpu.VMEM((MIX_ROWS, 2 * HALF_STATES), bf16),
        pltpu.VMEM((MIX_ROWS, W_S5), f32),
        pltpu.VMEM((BATCH, 2 * S5_GROUPS * S5_STATE), f32),
        pltpu.VMEM((MIX_ROWS, W_LRU), f32),
        pltpu.VMEM((MIX_ROWS, W_LRU), f32),
        pltpu.VMEM((MIX_ROWS, W_LRU), f32),
        pltpu.VMEM((BATCH, W_LRU), f32),
        pltpu.VMEM((MIX_ROWS, D_MODEL), bf16),
    ]
    x1 = pl.pallas_call(
        _mixer_kernel,
        grid=(SEQ // TIME_CHUNK,),
        in_specs=in_specs,
        out_specs=x_spec,
        out_shape=jax.ShapeDtypeStruct((BATCH, SEQ, D_MODEL), f32),
        scratch_shapes=scratch,
        compiler_params=pltpu.CompilerParams(
            dimension_semantics=("arbitrary",), vmem_limit_bytes=VMEM_LIMIT_BYTES),
        name="mixer",
    )(*mixer_inputs)
    return x1.reshape(BATCH * SEQ, D_MODEL), norm2_g, w_gate, w_up, w_down


def _ffn(x1, norm2_g, w_gate, w_up, w_down, final_g):
    f32 = jnp.float32
    bf16 = jnp.bfloat16
    rows = x1.shape[0]
    row_spec = pl.BlockSpec((FFN_ROWS, D_MODEL), lambda i: (i, 0))
    inputs = [x1, norm2_g.astype(f32).reshape(1, D_MODEL), w_gate.astype(bf16), w_up.astype(bf16),
              w_down.astype(bf16), final_g.astype(f32).reshape(1, D_MODEL)]
    return pl.pallas_call(
        _ffn_kernel,
        grid=(rows // FFN_ROWS,),
        in_specs=[row_spec] + [_const_spec(a.shape) for a in inputs[1:]],
        out_specs=row_spec,
        out_shape=jax.ShapeDtypeStruct((rows, D_MODEL), f32),
        scratch_shapes=[pltpu.VMEM((FFN_ROWS, D_FF), bf16)],
        compiler_params=pltpu.CompilerParams(
            dimension_semantics=("arbitrary",), vmem_limit_bytes=VMEM_LIMIT_BYTES),
        name="ffn",
    )(*inputs)


def kernel(x, norm1_g, w_in, s5_lambda_re, s5_lambda_im, s5_log_step, s5_b_re, s5_b_im, s5_c_re, s5_c_im, s5_d, s5_w_glu, s5_b_glu, lru_conv_w, lru_conv_b, lru_w_a, lru_b_a, lru_w_x, lru_b_x, lru_lambda, s5_out_g, lru_out_g, w_out, norm2_g, w_gate, w_up, w_down, final_g):
    x1, g2, wg, wu, wd = _layer(
        x, norm1_g[0], w_in[0], s5_lambda_re[0], s5_lambda_im[0], s5_log_step[0], s5_b_re[0],
        s5_b_im[0], s5_c_re[0], s5_c_im[0], s5_d[0], s5_w_glu[0], s5_b_glu[0], lru_conv_w[0],
        lru_conv_b[0], lru_w_a[0], lru_b_a[0], lru_w_x[0], lru_b_x[0], lru_lambda[0],
        s5_out_g[0], lru_out_g[0], w_out[0], norm2_g[0], w_gate[0], w_up[0], w_down[0])
    out = _ffn(x1, g2, wg, wu, wd, final_g)
    return out.reshape(BATCH, SEQ, D_MODEL)
```

```python
import jax
import jax.numpy as jnp
from jax import lax
from jax.experimental import pallas as pl
from jax.experimental.pallas import tpu as pltpu

D_MODEL = 1024
BATCH = 16
SEQ = 2048
W_S5 = 512
S5_GROUP_CH = 16
S5_GROUPS = 32
S5_STATE = 64
W_LRU = 512
LRU_HEADS = 8
LRU_HEAD_DIM = 64
CONV_WIDTH = 4
RG_C = 8.0
D_FF = 2816
EPS = 1e-6

MXU_TILE = 256
S5_HALVES = W_S5 // MXU_TILE
HALF_GROUPS = S5_GROUPS // S5_HALVES
HALF_STATES = HALF_GROUPS * S5_STATE
SCAN_SLAB = 512
LRU_HALF_HEADS = LRU_HEADS // 2

TIME_CHUNK = 32
NUM_CHUNKS = SEQ // TIME_CHUNK
ROWS = TIME_CHUNK * BATCH
HIST_ROWS = (CONV_WIDTH - 1) * BATCH
LANES = 128
LANE_CHUNKS = D_MODEL // LANES
SUBLANES = 8
STAGE_SEG = TIME_CHUNK + SUBLANES
STAGE_ROWS = BATCH * STAGE_SEG
FF_CHUNK = 256
FFN_ROWS = 1024
VMEM_LIMIT_BYTES = 56 * 1024 * 1024


def _rms(x, g):
    ms = jnp.mean(x * x, axis=-1, keepdims=True)
    return x * lax.rsqrt(ms + EPS) * g


def _bdot(a, b):
    return jnp.dot(a.astype(jnp.bfloat16), b, preferred_element_type=jnp.float32)


def _mixer_phases(x2d, o2d, g1_ref, win_ref, wb_ref, lam_ref, wc_ref, d_ref, wglu_ref,
                  bglu_ref, cw_ref, cb_ref, wg_ref, ba_ref, bx_ref, clog_ref, s5g_ref, lrug_ref,
                  wout_ref, xs, osb, xtb, hbuf, ubuf, xrbuf, yrbuf, bubuf, stbuf, ybuf, s5state,
                  convbuf, abuf, gbuf, hlru, lrustate, mixbuf):
    f32 = jnp.float32
    bf16 = jnp.bfloat16

    def norm_stage():
        g1 = g1_ref[...]
        for b in range(BATCH):
            hn = _rms(x2d[b * TIME_CHUNK:(b + 1) * TIME_CHUNK, :], g1)
            for j in range(LANE_CHUNKS):
                xs[j, b * STAGE_SEG:b * STAGE_SEG + TIME_CHUNK, :] = hn[:, j * LANES:(j + 1) * LANES]
        for t in range(TIME_CHUNK):
            ht = jnp.concatenate(
                [xs[j, pl.ds(t, BATCH, stride=STAGE_SEG), :] for j in range(LANE_CHUNKS)], axis=-1)
            hbuf[t * BATCH:(t + 1) * BATCH, :] = ht.astype(bf16)

    def in_proj():
        h = hbuf[...]
        ubuf[...] = jnp.dot(h, win_ref[:, 0:W_S5], preferred_element_type=f32)
        xrbuf[pl.ds(HIST_ROWS, ROWS), :] = jnp.dot(
            h, win_ref[:, W_S5:W_S5 + W_LRU], preferred_element_type=f32)
        yrbuf[...] = jnp.dot(h, win_ref[:, W_S5 + W_LRU:], preferred_element_type=f32)

    def b_proj(hf):
        def step():
            bubuf[hf] = _bdot(ubuf[:, hf * MXU_TILE:(hf + 1) * MXU_TILE], wb_ref[hf])
        return step

    def s5_scan(hf, q):
        def step():
            re0 = q * SCAN_SLAB
            im0 = HALF_STATES + q * SCAN_SLAB
            sbase = hf * 2 * HALF_STATES
            lr = jnp.broadcast_to(lam_ref[hf:hf + 1, re0:re0 + SCAN_SLAB], (BATCH, SCAN_SLAB))
            li = jnp.broadcast_to(lam_ref[hf:hf + 1, im0:im0 + SCAN_SLAB], (BATCH, SCAN_SLAB))
            sre = s5state[:, sbase + re0:sbase + re0 + SCAN_SLAB]
            sim = s5state[:, sbase + im0:sbase + im0 + SCAN_SLAB]
            for t in range(TIME_CHUNK):
                rows = slice(t * BATCH, (t + 1) * BATCH)
                bre = bubuf[hf, rows, re0:re0 + SCAN_SLAB]
                bim = bubuf[hf, rows, im0:im0 + SCAN_SLAB]
                sre, sim = lr * sre - li * sim + bre, lr * sim + li * sre + bim
                stbuf[hf, rows, re0:re0 + SCAN_SLAB] = sre.astype(bf16)
                stbuf[hf, rows, im0:im0 + SCAN_SLAB] = sim.astype(bf16)
            s5state[:, sbase + re0:sbase + re0 + SCAN_SLAB] = sre
            s5state[:, sbase + im0:sbase + im0 + SCAN_SLAB] = sim
        return step

    def c_proj(hf):
        def step():
            ybuf[:, hf * MXU_TILE:(hf + 1) * MXU_TILE] = jnp.dot(
                stbuf[hf], wc_ref[hf], preferred_element_type=f32)
        return step

    def s5_out():
        y = ybuf[...] + d_ref[...] * ubuf[...]
        z = jax.nn.gelu(y)
        z = z * jax.nn.sigmoid(_bdot(z, wglu_ref[...]) + bglu_ref[...])
        mixbuf[:, 0:W_S5] = _rms(z, s5g_ref[...]).astype(bf16)

    def lru_conv():
        convbuf[...] = cb_ref[...] + sum(
            xrbuf[pl.ds(k * BATCH, ROWS), :] * cw_ref[k:k + 1, :] for k in range(CONV_WIDTH))
        xrbuf[pl.ds(0, HIST_ROWS), :] = xrbuf[pl.ds(ROWS, HIST_ROWS), :]

    def lru_gates(hf):
        def step():
            c0 = hf * MXU_TILE
            cv = convbuf[:, c0:c0 + MXU_TILE]
            ri = _bdot(cv, wg_ref[hf])
            r = jax.nn.sigmoid(ri[:, 0:MXU_TILE] + ba_ref[:, c0:c0 + MXU_TILE])
            ig = jax.nn.sigmoid(ri[:, MXU_TILE:] + bx_ref[:, c0:c0 + MXU_TILE])
            log_a = clog_ref[:, c0:c0 + MXU_TILE] * r
            a = jnp.exp(log_a)
            abuf[:, c0:c0 + MXU_TILE] = a
            one_minus_a2 = -jnp.tanh(log_a) * (a * a + 1.0)
            gbuf[:, c0:c0 + MXU_TILE] = jnp.sqrt(one_minus_a2) * (ig * cv)
        return step

    def lru_scan():
        hstate = lrustate[...]
        for t in range(TIME_CHUNK):
            rows = slice(t * BATCH, (t + 1) * BATCH)
            hstate = abuf[rows, :] * hstate + gbuf[rows, :]
            hlru[rows, :] = hstate
        lrustate[...] = hstate

    def lru_out():
        out = hlru[...] * jax.nn.gelu(yrbuf[...])
        mixbuf[:, W_S5:] = _rms(out, lrug_ref[...]).astype(bf16)

    def out_proj():
        xtb[...] = jnp.dot(mixbuf[...], wout_ref[...], preferred_element_type=f32)

    def scatter():
        for t in range(TIME_CHUNK):
            row = xtb[t * BATCH:(t + 1) * BATCH, :]
            for j in range(LANE_CHUNKS):
                osb[j, pl.ds(t, BATCH, stride=STAGE_SEG), :] = row[:, j * LANES:(j + 1) * LANES]

    def residual():
        for b in range(BATCH):
            delta = jnp.concatenate(
                [osb[j, b * STAGE_SEG:b * STAGE_SEG + TIME_CHUNK, :] for j in range(LANE_CHUNKS)],
                axis=-1)
            rows = slice(b * TIME_CHUNK, (b + 1) * TIME_CHUNK)
            o2d[rows, :] = x2d[rows, :] + delta

    return dict(
        norm_stage=norm_stage, in_proj=in_proj,
        b_proj0=b_proj(0), scan00=s5_scan(0, 0), scan01=s5_scan(0, 1), c_proj0=c_proj(0),
        b_proj1=b_proj(1), scan10=s5_scan(1, 0), scan11=s5_scan(1, 1), c_proj1=c_proj(1),
        s5_out=s5_out, lru_conv=lru_conv, lru_gates0=lru_gates(0), lru_gates1=lru_gates(1),
        lru_scan=lru_scan, lru_out=lru_out, out_proj=out_proj, scatter=scatter,
        residual=residual)


_MIXER_ORDER = (
    "norm_stage", "in_proj",
    "b_proj0", "lru_conv", "scan00", "scan01",
    "c_proj0", "b_proj1", "lru_gates0", "scan10", "scan11",
    "c_proj1", "lru_gates1", "s5_out", "lru_scan", "lru_out",
    "out_proj", "scatter", "residual",
)


def _mixer_kernel(x_ref, g1_ref, win_ref, wb_ref, lam_ref, wc_ref, d_ref, wglu_ref, bglu_ref,
                  cw_ref, cb_ref, wg_ref, ba_ref, bx_ref, clog_ref, s5g_ref, lrug_ref, wout_ref,
                  o_ref,
                  xs, osb, xtb, hbuf, ubuf, xrbuf, yrbuf, bubuf, stbuf, ybuf, s5state,
                  convbuf, abuf, gbuf, hlru, lrustate, mixbuf):
    @pl.when(pl.program_id(0) == 0)
    def _():
        s5state[...] = jnp.zeros_like(s5state)
        lrustate[...] = jnp.zeros_like(lrustate)
        xrbuf[pl.ds(0, HIST_ROWS), :] = jnp.zeros((HIST_ROWS, W_LRU), jnp.float32)

    steps = _mixer_phases(
        x_ref.reshape(ROWS, D_MODEL), o_ref.reshape(ROWS, D_MODEL),
        g1_ref, win_ref, wb_ref, lam_ref, wc_ref, d_ref, wglu_ref, bglu_ref, cw_ref,
        cb_ref, wg_ref, ba_ref, bx_ref, clog_ref, s5g_ref, lrug_ref, wout_ref,
        xs, osb, xtb, hbuf, ubuf, xrbuf, yrbuf, bubuf, stbuf, ybuf, s5state,
        convbuf, abuf, gbuf, hlru, lrustate, mixbuf)
    assert sorted(steps) == sorted(_MIXER_ORDER)
    for name in _MIXER_ORDER:
        steps[name]()


def _ffn_kernel(x_ref, g2_ref, wgu_ref, wdown_ref, gf_ref, o_ref, actbuf):
    f32 = jnp.float32
    bf16 = jnp.bfloat16
    x = x_ref[...]
    h2 = _rms(x, g2_ref[...]).astype(bf16)
    for c in range(D_FF // FF_CHUNK):
        gu = jnp.dot(h2, wgu_ref[:, 2 * c * FF_CHUNK:2 * (c + 1) * FF_CHUNK],
                     preferred_element_type=f32)
        for k in range(FF_CHUNK // LANES):
            gate = gu[:, 2 * k * LANES:(2 * k + 1) * LANES]
            up = gu[:, (2 * k + 1) * LANES:(2 * k + 2) * LANES]
            c0 = c * FF_CHUNK + k * LANES
            actbuf[:, c0:c0 + LANES] = (jax.nn.silu(gate) * up).astype(bf16)
    x2 = x + jnp.dot(actbuf[...], wdown_ref[...], preferred_element_type=f32)
    o_ref[...] = _rms(x2, gf_ref[...])


def _const_spec(shape):
    nd = len(shape)
    return pl.BlockSpec(shape, lambda i, _nd=nd: (0,) * _nd, pipeline_mode=pl.Buffered(1))


def _block_diag(blocks):
    p, n, r, c = blocks.shape
    eye = jnp.eye(n, dtype=blocks.dtype)
    return jnp.einsum('pkrc,kj->pkrjc', blocks, eye).reshape(p, n * r, n * c)


def _prepare_s5(lam_re, lam_im, log_step, b_re, b_im, c_re, c_im):
    f32 = jnp.float32
    la = jnp.minimum(lam_re.astype(f32), -1e-4)
    lb = lam_im.astype(f32)
    step = jnp.exp(log_step.astype(f32))[:, None]
    mag = jnp.exp(la * step)
    lbar_re = mag * jnp.cos(lb * step)
    lbar_im = mag * jnp.sin(lb * step)
    den = la * la + lb * lb
    q_re = (((lbar_re - 1.0) * la + lbar_im * lb) / den)[..., None]
    q_im = ((lbar_im * la - (lbar_re - 1.0) * lb) / den)[..., None]
    bre = b_re.astype(f32)
    bim = b_im.astype(f32)
    bbar_re = q_re * bre - q_im * bim
    bbar_im = q_re * bim + q_im * bre

    def b_half(v):
        v = v.reshape(S5_HALVES, HALF_GROUPS, S5_STATE, S5_GROUP_CH).transpose(0, 1, 3, 2)
        return _block_diag(v)

    def c_half(v):
        v = v.reshape(S5_HALVES, HALF_GROUPS, S5_GROUP_CH, S5_STATE).transpose(0, 1, 3, 2)
        return _block_diag(v)

    wb = jnp.concatenate([b_half(bbar_re), b_half(bbar_im)], axis=2)
    wc = jnp.concatenate([c_half(c_re.astype(f32)), -c_half(c_im.astype(f32))], axis=1)
    lam_v = jnp.concatenate([lbar_re.reshape(S5_HALVES, HALF_STATES),
                             lbar_im.reshape(S5_HALVES, HALF_STATES)], axis=1)
    return wb.astype(jnp.bfloat16), wc.astype(jnp.bfloat16), lam_v


def _prepare_lru_gates(w_a, w_x):
    def half(w):
        return _block_diag(w.astype(jnp.float32).reshape(2, LRU_HALF_HEADS, LRU_HEAD_DIM, LRU_HEAD_DIM))
    return jnp.concatenate([half(w_a), half(w_x)], axis=2).astype(jnp.bfloat16)


def _mixer(x, norm1_g, w_in, s5_lambda_re, s5_lambda_im, s5_log_step, s5_b_re, s5_b_im,
           s5_c_re, s5_c_im, s5_d, s5_w_glu, s5_b_glu, lru_conv_w, lru_conv_b, lru_w_a, lru_b_a,
           lru_w_x, lru_b_x, lru_lambda, s5_out_g, lru_out_g, w_out):
    f32 = jnp.float32
    bf16 = jnp.bfloat16
    wb, wc, lam_v = _prepare_s5(s5_lambda_re, s5_lambda_im, s5_log_step, s5_b_re, s5_b_im,
                                s5_c_re, s5_c_im)
    wg = _prepare_lru_gates(lru_w_a, lru_w_x)
    clog = (RG_C * jax.nn.log_sigmoid(lru_lambda.astype(f32))).reshape(1, W_LRU)
    row = lambda v, n: v.astype(f32).reshape(1, n)

    inputs = [
        x,
        row(norm1_g, D_MODEL),
        w_in.astype(bf16),
        wb, lam_v, wc,
        row(s5_d, W_S5),
        s5_w_glu.astype(bf16),
        row(s5_b_glu, W_S5),
        lru_conv_w.astype(f32),
        row(lru_conv_b, W_LRU),
        wg,
        row(lru_b_a, W_LRU),
        row(lru_b_x, W_LRU),
        clog,
        row(s5_out_g, W_S5),
        row(lru_out_g, W_LRU),
        w_out.astype(bf16),
    ]
    x_spec = pl.BlockSpec((BATCH, TIME_CHUNK, D_MODEL), lambda i: (0, i, 0))
    in_specs = [x_spec] + [_const_spec(a.shape) for a in inputs[1:]]
    scratch = [
        pltpu.VMEM((LANE_CHUNKS, STAGE_ROWS, LANES), f32),
        pltpu.VMEM((LANE_CHUNKS, STAGE_ROWS, LANES), f32),
        pltpu.VMEM((ROWS, D_MODEL), f32),
        pltpu.VMEM((ROWS, D_MODEL), bf16),
        pltpu.VMEM((ROWS, W_S5), f32),
        pltpu.VMEM((ROWS + HIST_ROWS, W_LRU), f32),
        pltpu.VMEM((ROWS, W_LRU), f32),
        pltpu.VMEM((S5_HALVES, ROWS, 2 * HALF_STATES), f32),
        pltpu.VMEM((S5_HALVES, ROWS, 2 * HALF_STATES), bf16),
        pltpu.VMEM((ROWS, W_S5), f32),
        pltpu.VMEM((BATCH, 2 * S5_GROUPS * S5_STATE), f32),
        pltpu.VMEM((ROWS, W_LRU), f32),
        pltpu.VMEM((ROWS, W_LRU), f32),
        pltpu.VMEM((ROWS, W_LRU), f32),
        pltpu.VMEM((ROWS, W_LRU), f32),
        pltpu.VMEM((BATCH, W_LRU), f32),
        pltpu.VMEM((ROWS, D_MODEL), bf16),
    ]
    return pl.pallas_call(
        _mixer_kernel,
        grid=(NUM_CHUNKS,),
        in_specs=in_specs,
        out_specs=x_spec,
        out_shape=jax.ShapeDtypeStruct((BATCH, SEQ, D_MODEL), f32),
        scratch_shapes=scratch,
        compiler_params=pltpu.CompilerParams(
            dimension_semantics=("arbitrary",), vmem_limit_bytes=VMEM_LIMIT_BYTES),
        name="mixer",
    )(*inputs)


def _ffn(x1, norm2_g, w_gate, w_up, w_down, final_g):
    f32 = jnp.float32
    bf16 = jnp.bfloat16
    rows = x1.shape[0]
    row_spec = pl.BlockSpec((FFN_ROWS, D_MODEL), lambda i: (i, 0))
    blocks = D_FF // LANES
    wgu = jnp.stack([w_gate.astype(bf16).reshape(D_MODEL, blocks, LANES),
                     w_up.astype(bf16).reshape(D_MODEL, blocks, LANES)], axis=2)
    inputs = [x1, norm2_g.astype(f32).reshape(1, D_MODEL), wgu.reshape(D_MODEL, 2 * D_FF),
              w_down.astype(bf16), final_g.astype(f32).reshape(1, D_MODEL)]
    return pl.pallas_call(
        _ffn_kernel,
        grid=(rows // FFN_ROWS,),
        in_specs=[row_spec] + [_const_spec(a.shape) for a in inputs[1:]],
        out_specs=row_spec,
        out_shape=jax.ShapeDtypeStruct((rows, D_MODEL), f32),
        scratch_shapes=[pltpu.VMEM((FFN_ROWS, D_FF), bf16)],
        compiler_params=pltpu.CompilerParams(
            dimension_semantics=("arbitrary",), vmem_limit_bytes=VMEM_LIMIT_BYTES),
        name="ffn",
    )(*inputs)


def kernel(x, norm1_g, w_in, s5_lambda_re, s5_lambda_im, s5_log_step, s5_b_re, s5_b_im, s5_c_re, s5_c_im, s5_d, s5_w_glu, s5_b_glu, lru_conv_w, lru_conv_b, lru_w_a, lru_b_a, lru_w_x, lru_b_x, lru_lambda, s5_out_g, lru_out_g, w_out, norm2_g, w_gate, w_up, w_down, final_g):
    x1 = _mixer(
        x, norm1_g[0], w_in[0], s5_lambda_re[0], s5_lambda_im[0], s5_log_step[0], s5_b_re[0],
        s5_b_im[0], s5_c_re[0], s5_c_im[0], s5_d[0], s5_w_glu[0], s5_b_glu[0], lru_conv_w[0],
        lru_conv_b[0], lru_w_a[0], lru_b_a[0], lru_w_x[0], lru_b_x[0], lru_lambda[0],
        s5_out_g[0], lru_out_g[0], w_out[0])
    out = _ffn(x1.reshape(BATCH * SEQ, D_MODEL), norm2_g[0], w_gate[0], w_up[0], w_down[0],
               final_g)
    return out.reshape(BATCH, SEQ, D_MODEL)
```

```python
import jax
import jax.numpy as jnp
from jax import lax
from jax.experimental import pallas as pl
from jax.experimental.pallas import tpu as pltpu

D_MODEL = 1024
BATCH = 16
SEQ = 2048
W_S5 = 512
S5_GROUP_CH = 16
S5_GROUPS = 32
S5_STATE = 64
W_LRU = 512
LRU_HEADS = 8
LRU_HEAD_DIM = 64
CONV_WIDTH = 4
RG_C = 8.0
D_FF = 2816
EPS = 1e-6

MXU_TILE = 256
S5_HALVES = W_S5 // MXU_TILE
HALF_GROUPS = S5_GROUPS // S5_HALVES
HALF_STATES = HALF_GROUPS * S5_STATE
SCAN_SLAB = 512
LRU_HALF_HEADS = LRU_HEADS // 2

TIME_CHUNK = 32
NUM_CHUNKS = SEQ // TIME_CHUNK
ROWS = TIME_CHUNK * BATCH
HIST_ROWS = (CONV_WIDTH - 1) * BATCH
LANES = 128
LANE_CHUNKS = D_MODEL // LANES
SUBLANES = 8
STAGE_SEG = TIME_CHUNK + SUBLANES
STAGE_ROWS = BATCH * STAGE_SEG
FF_CHUNK = 256
FFN_ROWS = 1024
VMEM_LIMIT_BYTES = 56 * 1024 * 1024
VEC_G1 = 0
VEC_D_BGLU = 1
VEC_CB_BA = 2
VEC_BX_CLOG = 3
VEC_S5G_LRUG = 4
VEC_CW = 5
VEC_LAM = 7
VEC_ROWS = 16


def _rms(x, g):
    ms = jnp.mean(x * x, axis=-1, keepdims=True)
    return x * lax.rsqrt(ms + EPS) * g


def _bdot(a, b):
    return jnp.dot(a.astype(jnp.bfloat16), b, preferred_element_type=jnp.float32)


def _mixer_phases(xprev2d, xnext2d, o2d, cur, vec_ref, win_ref, wb_ref, wc_ref, wglu_ref, wg_ref,
                  wout_ref, xs, osb, xtb, hbuf, ubuf, xrbuf, yrbuf, bubuf, stbuf, ybuf, s5state,
                  convbuf, abuf, gbuf, hlru, lrustate, mixbuf):
    f32 = jnp.float32
    bf16 = jnp.bfloat16
    nxt = 1 - cur

    def vec(row, col0, width):
        return vec_ref[row:row + 1, col0:col0 + width]

    def norm_stage(src2d):
        g1 = vec(VEC_G1, 0, D_MODEL)
        for b in range(BATCH):
            hn = _rms(src2d[b * TIME_CHUNK:(b + 1) * TIME_CHUNK, :], g1)
            for j in range(LANE_CHUNKS):
                xs[j, b * STAGE_SEG:b * STAGE_SEG + TIME_CHUNK, :] = hn[:, j * LANES:(j + 1) * LANES]
        for t in range(TIME_CHUNK):
            ht = jnp.concatenate(
                [xs[j, pl.ds(t, BATCH, stride=STAGE_SEG), :] for j in range(LANE_CHUNKS)], axis=-1)
            hbuf[t * BATCH:(t + 1) * BATCH, :] = ht.astype(bf16)

    def in_proj_u(dst):
        ubuf[dst] = jnp.dot(hbuf[...], win_ref[:, 0:W_S5], preferred_element_type=f32)

    def in_proj_xr(dst):
        xrbuf[dst, pl.ds(HIST_ROWS, ROWS), :] = jnp.dot(
            hbuf[...], win_ref[:, W_S5:W_S5 + W_LRU], preferred_element_type=f32)

    def in_proj_yr(dst):
        yrbuf[dst] = jnp.dot(hbuf[...], win_ref[:, W_S5 + W_LRU:], preferred_element_type=f32)

    def first_chunk():
        norm_stage(xprev2d)
        in_proj_u(0)
        in_proj_xr(0)
        in_proj_yr(0)

    def b_proj(hf):
        def step():
            bubuf[hf] = _bdot(ubuf[cur, :, hf * MXU_TILE:(hf + 1) * MXU_TILE], wb_ref[hf])
        return step

    def s5_scan(hf, q):
        def step():
            re0 = q * SCAN_SLAB
            im0 = HALF_STATES + q * SCAN_SLAB
            sbase = hf * 2 * HALF_STATES
            lr = jnp.broadcast_to(vec(VEC_LAM + 2 * hf, re0, SCAN_SLAB), (BATCH, SCAN_SLAB))
            li = jnp.broadcast_to(vec(VEC_LAM + 2 * hf + 1, re0, SCAN_SLAB), (BATCH, SCAN_SLAB))
            sre = s5state[:, sbase + re0:sbase + re0 + SCAN_SLAB]
            sim = s5state[:, sbase + im0:sbase + im0 + SCAN_SLAB]
            for t in range(TIME_CHUNK):
                rows = slice(t * BATCH, (t + 1) * BATCH)
                bre = bubuf[hf, rows, re0:re0 + SCAN_SLAB]
                bim = bubuf[hf, rows, im0:im0 + SCAN_SLAB]
                sre, sim = lr * sre - li * sim + bre, lr * sim + li * sre + bim
                stbuf[hf, rows, re0:re0 + SCAN_SLAB] = sre.astype(bf16)
                stbuf[hf, rows, im0:im0 + SCAN_SLAB] = sim.astype(bf16)
            s5state[:, sbase + re0:sbase + re0 + SCAN_SLAB] = sre
            s5state[:, sbase + im0:sbase + im0 + SCAN_SLAB] = sim
        return step

    def c_proj(hf):
        def step():
            ybuf[:, hf * MXU_TILE:(hf + 1) * MXU_TILE] = jnp.dot(
                stbuf[hf], wc_ref[hf], preferred_element_type=f32)
        return step

    def s5_out():
        y = ybuf[...] + vec(VEC_D_BGLU, 0, W_S5) * ubuf[cur]
        z = jax.nn.gelu(y)
        z = z * jax.nn.sigmoid(_bdot(z, wglu_ref[...]) + vec(VEC_D_BGLU, W_S5, W_S5))
        mixbuf[:, 0:W_S5] = _rms(z, vec(VEC_S5G_LRUG, 0, W_S5)).astype(bf16)

    def lru_conv():
        convbuf[...] = vec(VEC_CB_BA, 0, W_LRU) + sum(
            xrbuf[cur, pl.ds(k * BATCH, ROWS), :] * vec(VEC_CW + k // 2, (k % 2) * W_LRU, W_LRU)
            for k in range(CONV_WIDTH))
        xrbuf[nxt, pl.ds(0, HIST_ROWS), :] = xrbuf[cur, pl.ds(ROWS, HIST_ROWS), :]

    def lru_gates(hf):
        def step():
            c0 = hf * MXU_TILE
            cv = convbuf[:, c0:c0 + MXU_TILE]
            ri = _bdot(cv, wg_ref[hf])
            r = jax.nn.sigmoid(ri[:, 0:MXU_TILE] + vec(VEC_CB_BA, W_LRU + c0, MXU_TILE))
            ig = jax.nn.sigmoid(ri[:, MXU_TILE:] + vec(VEC_BX_CLOG, c0, MXU_TILE))
            log_a = vec(VEC_BX_CLOG, W_LRU + c0, MXU_TILE) * r
            a = jnp.exp(log_a)
            abuf[:, c0:c0 + MXU_TILE] = a
            one_minus_a2 = -jnp.tanh(log_a) * (a * a + 1.0)
            gbuf[:, c0:c0 + MXU_TILE] = jnp.sqrt(one_minus_a2) * (ig * cv)
        return step

    def lru_scan():
        hstate = lrustate[...]
        for t in range(TIME_CHUNK):
            rows = slice(t * BATCH, (t + 1) * BATCH)
            hstate = abuf[rows, :] * hstate + gbuf[rows, :]
            hlru[rows, :] = hstate
        lrustate[...] = hstate

    def lru_out():
        out = hlru[...] * jax.nn.gelu(yrbuf[cur])
        mixbuf[:, W_S5:] = _rms(out, vec(VEC_S5G_LRUG, W_S5, W_LRU)).astype(bf16)

    def out_proj():
        xtb[...] = jnp.dot(mixbuf[...], wout_ref[...], preferred_element_type=f32)

    def scatter():
        for t in range(TIME_CHUNK):
            row = xtb[t * BATCH:(t + 1) * BATCH, :]
            for j in range(LANE_CHUNKS):
                osb[j, pl.ds(t, BATCH, stride=STAGE_SEG), :] = row[:, j * LANES:(j + 1) * LANES]

    def residual():
        for b in range(BATCH):
            delta = jnp.concatenate(
                [osb[j, b * STAGE_SEG:b * STAGE_SEG + TIME_CHUNK, :] for j in range(LANE_CHUNKS)],
                axis=-1)
            rows = slice(b * TIME_CHUNK, (b + 1) * TIME_CHUNK)
            o2d[rows, :] = xprev2d[rows, :] + delta

    return dict(
        first_chunk=first_chunk, norm_next=lambda: norm_stage(xnext2d),
        in_proj_u=lambda: in_proj_u(nxt), in_proj_xr=lambda: in_proj_xr(nxt),
        in_proj_yr=lambda: in_proj_yr(nxt),
        b_proj0=b_proj(0), scan00=s5_scan(0, 0), scan01=s5_scan(0, 1), c_proj0=c_proj(0),
        b_proj1=b_proj(1), scan10=s5_scan(1, 0), scan11=s5_scan(1, 1), c_proj1=c_proj(1),
        s5_out=s5_out, lru_conv=lru_conv, lru_gates0=lru_gates(0), lru_gates1=lru_gates(1),
        lru_scan=lru_scan, lru_out=lru_out, out_proj=out_proj, scatter=scatter,
        residual=residual)


_MIXER_ORDER = (
    "b_proj0", "norm_next",
    "b_proj1", "lru_conv",
    "out_proj", "scan00", "scan01",
    "c_proj0", "scatter", "residual",
    "in_proj_u", "lru_gates0",
    "in_proj_xr", "scan10",
    "in_proj_yr", "scan11",
    "c_proj1", "lru_gates1",
    "s5_out", "lru_scan", "lru_out",
)


def _mixer_kernel(xprev_ref, xnext_ref, vec_ref, win_ref, wb_ref, wc_ref, wglu_ref, wg_ref,
                  wout_ref,
                  o_ref,
                  xs, osb, xtb, hbuf, ubuf, xrbuf, yrbuf, bubuf, stbuf, ybuf, s5state,
                  convbuf, abuf, gbuf, hlru, lrustate, mixbuf):
    s = pl.program_id(0)
    steps = _mixer_phases(
        xprev_ref.reshape(ROWS, D_MODEL), xnext_ref.reshape(ROWS, D_MODEL),
        o_ref.reshape(ROWS, D_MODEL), lax.rem(s, 2),
        vec_ref, win_ref, wb_ref, wc_ref, wglu_ref, wg_ref, wout_ref,
        xs, osb, xtb, hbuf, ubuf, xrbuf, yrbuf, bubuf, stbuf, ybuf, s5state,
        convbuf, abuf, gbuf, hlru, lrustate, mixbuf)

    @pl.when(s == 0)
    def _():
        s5state[...] = jnp.zeros_like(s5state)
        lrustate[...] = jnp.zeros_like(lrustate)
        xrbuf[0, pl.ds(0, HIST_ROWS), :] = jnp.zeros((HIST_ROWS, W_LRU), jnp.float32)
        mixbuf[...] = jnp.zeros_like(mixbuf)
        steps["first_chunk"]()

    assert sorted(steps) == sorted(_MIXER_ORDER + ("first_chunk",))
    for name in _MIXER_ORDER:
        steps[name]()


def _ffn_kernel(x_ref, vec_ref, wgate_ref, wup_ref, wdown_ref, o_ref, actbuf):
    f32 = jnp.float32
    bf16 = jnp.bfloat16
    x = x_ref[...]
    h2 = _rms(x, vec_ref[0:1, :]).astype(bf16)
    for c in range(D_FF // FF_CHUNK):
        c0 = c * FF_CHUNK
        gate = jnp.dot(h2, wgate_ref[:, c0:c0 + FF_CHUNK], preferred_element_type=f32)
        up = jnp.dot(h2, wup_ref[:, c0:c0 + FF_CHUNK], preferred_element_type=f32)
        actbuf[:, c0:c0 + FF_CHUNK] = (jax.nn.silu(gate) * up).astype(bf16)
    x2 = x + jnp.dot(actbuf[...], wdown_ref[...], preferred_element_type=f32)
    o_ref[...] = _rms(x2, vec_ref[1:2, :])


def _const_spec(shape):
    nd = len(shape)
    return pl.BlockSpec(shape, lambda i, _nd=nd: (0,) * _nd, pipeline_mode=pl.Buffered(1))


def _block_diag(blocks):
    p, n, r, c = blocks.shape
    eye = jnp.eye(n, dtype=blocks.dtype)
    return jnp.einsum('pkrc,kj->pkrjc', blocks, eye).reshape(p, n * r, n * c)


def _prepare_s5(lam_re, lam_im, log_step, b_re, b_im, c_re, c_im):
    f32 = jnp.float32
    la = jnp.minimum(lam_re.astype(f32), -1e-4)
    lb = lam_im.astype(f32)
    step = jnp.exp(log_step.astype(f32))[:, None]
    mag = jnp.exp(la * step)
    lbar_re = mag * jnp.cos(lb * step)
    lbar_im = mag * jnp.sin(lb * step)
    den = la * la + lb * lb
    q_re = (((lbar_re - 1.0) * la + lbar_im * lb) / den)[..., None]
    q_im = ((lbar_im * la - (lbar_re - 1.0) * lb) / den)[..., None]
    bre = b_re.astype(f32)
    bim = b_im.astype(f32)
    bbar_re = q_re * bre - q_im * bim
    bbar_im = q_re * bim + q_im * bre

    def b_half(v):
        v = v.reshape(S5_HALVES, HALF_GROUPS, S5_STATE, S5_GROUP_CH).transpose(0, 1, 3, 2)
        return _block_diag(v)

    def c_half(v):
        v = v.reshape(S5_HALVES, HALF_GROUPS, S5_GROUP_CH, S5_STATE).transpose(0, 1, 3, 2)
        return _block_diag(v)

    wb = jnp.concatenate([b_half(bbar_re), b_half(bbar_im)], axis=2)
    wc = jnp.concatenate([c_half(c_re.astype(f32)), -c_half(c_im.astype(f32))], axis=1)
    lam_v = jnp.concatenate([lbar_re.reshape(S5_HALVES, HALF_STATES),
                             lbar_im.reshape(S5_HALVES, HALF_STATES)], axis=1)
    return wb.astype(jnp.bfloat16), wc.astype(jnp.bfloat16), lam_v


def _prepare_lru_gates(w_a, w_x):
    def half(w):
        return _block_diag(w.astype(jnp.float32).reshape(2, LRU_HALF_HEADS, LRU_HEAD_DIM, LRU_HEAD_DIM))
    return jnp.concatenate([half(w_a), half(w_x)], axis=2).astype(jnp.bfloat16)


def _mixer(x, norm1_g, w_in, s5_lambda_re, s5_lambda_im, s5_log_step, s5_b_re, s5_b_im,
           s5_c_re, s5_c_im, s5_d, s5_w_glu, s5_b_glu, lru_conv_w, lru_conv_b, lru_w_a, lru_b_a,
           lru_w_x, lru_b_x, lru_lambda, s5_out_g, lru_out_g, w_out):
    f32 = jnp.float32
    bf16 = jnp.bfloat16
    wb, wc, lam_v = _prepare_s5(s5_lambda_re, s5_lambda_im, s5_log_step, s5_b_re, s5_b_im,
                                s5_c_re, s5_c_im)
    wg = _prepare_lru_gates(lru_w_a, lru_w_x)
    clog = RG_C * jax.nn.log_sigmoid(lru_lambda.astype(f32))
    pair = lambda a, b: jnp.concatenate([a.astype(f32).reshape(-1), b.astype(f32).reshape(-1)])
    vec_rows = jnp.concatenate([
        norm1_g.astype(f32).reshape(1, D_MODEL),
        jnp.stack([pair(s5_d, s5_b_glu), pair(lru_conv_b, lru_b_a), pair(lru_b_x, clog),
                   pair(s5_out_g, lru_out_g)]),
        lru_conv_w.astype(f32).reshape(CONV_WIDTH // 2, D_MODEL),
        lam_v.reshape(2 * S5_HALVES, HALF_STATES),
    ])
    vecs = jnp.zeros((VEC_ROWS, D_MODEL), f32).at[:vec_rows.shape[0]].set(vec_rows)

    inputs = [x, x, vecs, w_in.astype(bf16), wb, wc, s5_w_glu.astype(bf16), wg, w_out.astype(bf16)]
    block = (BATCH, TIME_CHUNK, D_MODEL)
    prev_spec = pl.BlockSpec(block, lambda s: (0, jnp.maximum(s - 1, 0), 0))
    next_spec = pl.BlockSpec(block, lambda s: (0, jnp.minimum(s + 1, NUM_CHUNKS - 1), 0))
    in_specs = [prev_spec, next_spec] + [_const_spec(a.shape) for a in inputs[2:]]
    scratch = [
        pltpu.VMEM((LANE_CHUNKS, STAGE_ROWS, LANES), f32),
        pltpu.VMEM((LANE_CHUNKS, STAGE_ROWS, LANES), f32),
        pltpu.VMEM((ROWS, D_MODEL), f32),
        pltpu.VMEM((ROWS, D_MODEL), bf16),
        pltpu.VMEM((2, ROWS, W_S5), f32),
        pltpu.VMEM((2, ROWS + HIST_ROWS, W_LRU), f32),
        pltpu.VMEM((2, ROWS, W_LRU), f32),
        pltpu.VMEM((S5_HALVES, ROWS, 2 * HALF_STATES), f32),
        pltpu.VMEM((S5_HALVES, ROWS, 2 * HALF_STATES), bf16),
        pltpu.VMEM((ROWS, W_S5), f32),
        pltpu.VMEM((BATCH, 2 * S5_GROUPS * S5_STATE), f32),
        pltpu.VMEM((ROWS, W_LRU), f32),
        pltpu.VMEM((ROWS, W_LRU), f32),
        pltpu.VMEM((ROWS, W_LRU), f32),
        pltpu.VMEM((ROWS, W_LRU), f32),
        pltpu.VMEM((BATCH, W_LRU), f32),
        pltpu.VMEM((ROWS, D_MODEL), bf16),
    ]
    return pl.pallas_call(
        _mixer_kernel,
        grid=(NUM_CHUNKS + 1,),
        in_specs=in_specs,
        out_specs=prev_spec,
        out_shape=jax.ShapeDtypeStruct((BATCH, SEQ, D_MODEL), f32),
        scratch_shapes=scratch,
        compiler_params=pltpu.CompilerParams(
            dimension_semantics=("arbitrary",), vmem_limit_bytes=VMEM_LIMIT_BYTES),
        name="mixer",
    )(*inputs)


def _ffn(x1, norm2_g, w_gate, w_up, w_down, final_g):
    f32 = jnp.float32
    bf16 = jnp.bfloat16
    rows = x1.shape[0]
    row_spec = pl.BlockSpec((FFN_ROWS, D_MODEL), lambda i: (i, 0))
    vecs = jnp.zeros((SUBLANES, D_MODEL), f32).at[0].set(norm2_g.astype(f32)).at[1].set(
        final_g.astype(f32))
    inputs = [x1, vecs, w_gate.astype(bf16), w_up.astype(bf16), w_down.astype(bf16)]
    return pl.pallas_call(
        _ffn_kernel,
        grid=(rows // FFN_ROWS,),
        in_specs=[row_spec] + [_const_spec(a.shape) for a in inputs[1:]],
        out_specs=row_spec,
        out_shape=jax.ShapeDtypeStruct((rows, D_MODEL), f32),
        scratch_shapes=[pltpu.VMEM((FFN_ROWS, D_FF), bf16)],
        compiler_params=pltpu.CompilerParams(
            dimension_semantics=("arbitrary",), vmem_limit_bytes=VMEM_LIMIT_BYTES),
        name="ffn",
    )(*inputs)


def kernel(x, norm1_g, w_in, s5_lambda_re, s5_lambda_im, s5_log_step, s5_b_re, s5_b_im, s5_c_re, s5_c_im, s5_d, s5_w_glu, s5_b_glu, lru_conv_w, lru_conv_b, lru_w_a, lru_b_a, lru_w_x, lru_b_x, lru_lambda, s5_out_g, lru_out_g, w_out, norm2_g, w_gate, w_up, w_down, final_g):
    x1 = _mixer(
        x, norm1_g[0], w_in[0], s5_lambda_re[0], s5_lambda_im[0], s5_log_step[0], s5_b_re[0],
        s5_b_im[0], s5_c_re[0], s5_c_im[0], s5_d[0], s5_w_glu[0], s5_b_glu[0], lru_conv_w[0],
        lru_conv_b[0], lru_w_a[0], lru_b_a[0], lru_w_x[0], lru_b_x[0], lru_lambda[0],
        s5_out_g[0], lru_out_g[0], w_out[0])
    out = _ffn(x1.reshape(BATCH * SEQ, D_MODEL), norm2_g[0], w_gate[0], w_up[0], w_down[0],
               final_g)
    return out.reshape(BATCH, SEQ, D_MODEL)
```

```python
import jax
import jax.numpy as jnp
from jax import lax
from jax.experimental import pallas as pl
from jax.experimental.pallas import tpu as pltpu

D_MODEL = 1024
BATCH = 16
SEQ = 2048
W_S5 = 512
S5_GROUP_CH = 16
S5_GROUPS = 32
S5_STATE = 64
W_LRU = 512
LRU_HEADS = 8
LRU_HEAD_DIM = 64
CONV_WIDTH = 4
RG_C = 8.0
D_FF = 2816
EPS = 1e-6

MXU_TILE = 256
S5_HALVES = W_S5 // MXU_TILE
HALF_GROUPS = S5_GROUPS // S5_HALVES
HALF_STATES = HALF_GROUPS * S5_STATE
SCAN_SLAB = 512
LRU_HALF_HEADS = LRU_HEADS // 2

TIME_CHUNK = 32
NUM_CHUNKS = SEQ // TIME_CHUNK
ROWS = TIME_CHUNK * BATCH
HIST_ROWS = (CONV_WIDTH - 1) * BATCH
LANES = 128
LANE_CHUNKS = D_MODEL // LANES
SUBLANES = 8
STAGE_SEG = TIME_CHUNK + SUBLANES
STAGE_ROWS = BATCH * STAGE_SEG
FF_CHUNK = 256
FFN_ROWS = 1024
FFN_PARTS = 4
VMEM_LIMIT_BYTES = 56 * 1024 * 1024
VEC_G1 = 0
VEC_D_BGLU = 1
VEC_CB_BA = 2
VEC_BX_CLOG = 3
VEC_S5G_LRUG = 4
VEC_CW = 5
VEC_LAM = 7
VEC_ROWS = 16


def _rms(x, g):
    ms = jnp.mean(x * x, axis=-1, keepdims=True)
    return x * lax.rsqrt(ms + EPS) * g


def _bdot(a, b):
    return jnp.dot(a.astype(jnp.bfloat16), b, preferred_element_type=jnp.float32)


def _mixer_phases(x2d, o2d, vec_ref, win_ref, wb_ref, wc_ref, wglu_ref, wg_ref,
                  wout_ref, xs, osb, xtb, hbuf, ubuf, xrbuf, yrbuf, bubuf, stbuf, ybuf, s5state,
                  convbuf, abuf, gbuf, hlru, lrustate, mixbuf):
    f32 = jnp.float32
    bf16 = jnp.bfloat16

    def vec(row, col0, width):
        return vec_ref[row:row + 1, col0:col0 + width]

    def norm_stage():
        g1 = vec(VEC_G1, 0, D_MODEL)
        for b in range(BATCH):
            hn = _rms(x2d[b * TIME_CHUNK:(b + 1) * TIME_CHUNK, :], g1)
            for j in range(LANE_CHUNKS):
                xs[j, b * STAGE_SEG:b * STAGE_SEG + TIME_CHUNK, :] = hn[:, j * LANES:(j + 1) * LANES]
        for t in range(TIME_CHUNK):
            ht = jnp.concatenate(
                [xs[j, pl.ds(t, BATCH, stride=STAGE_SEG), :] for j in range(LANE_CHUNKS)], axis=-1)
            hbuf[t * BATCH:(t + 1) * BATCH, :] = ht.astype(bf16)

    def in_proj():
        h = hbuf[...]
        ubuf[...] = jnp.dot(h, win_ref[:, 0:W_S5], preferred_element_type=f32)
        xrbuf[pl.ds(HIST_ROWS, ROWS), :] = jnp.dot(
            h, win_ref[:, W_S5:W_S5 + W_LRU], preferred_element_type=f32)
        yrbuf[...] = jnp.dot(h, win_ref[:, W_S5 + W_LRU:], preferred_element_type=f32)

    def b_proj(hf):
        def step():
            bubuf[hf] = _bdot(ubuf[:, hf * MXU_TILE:(hf + 1) * MXU_TILE], wb_ref[hf])
        return step

    def s5_scan(hf, q):
        def step():
            re0 = q * SCAN_SLAB
            im0 = HALF_STATES + q * SCAN_SLAB
            sbase = hf * 2 * HALF_STATES
            lr = jnp.broadcast_to(vec(VEC_LAM + 2 * hf, re0, SCAN_SLAB), (BATCH, SCAN_SLAB))
            li = jnp.broadcast_to(vec(VEC_LAM + 2 * hf + 1, re0, SCAN_SLAB), (BATCH, SCAN_SLAB))
            sre = s5state[:, sbase + re0:sbase + re0 + SCAN_SLAB]
            sim = s5state[:, sbase + im0:sbase + im0 + SCAN_SLAB]
            for t in range(TIME_CHUNK):
                rows = slice(t * BATCH, (t + 1) * BATCH)
                bre = bubuf[hf, rows, re0:re0 + SCAN_SLAB]
                bim = bubuf[hf, rows, im0:im0 + SCAN_SLAB]
                sre, sim = lr * sre - li * sim + bre, lr * sim + li * sre + bim
                stbuf[hf, rows, re0:re0 + SCAN_SLAB] = sre.astype(bf16)
                stbuf[hf, rows, im0:im0 + SCAN_SLAB] = sim.astype(bf16)
            s5state[:, sbase + re0:sbase + re0 + SCAN_SLAB] = sre
            s5state[:, sbase + im0:sbase + im0 + SCAN_SLAB] = sim
        return step

    def s5_scan_half(hf):
        def step():
            for q in range(HALF_STATES // SCAN_SLAB):
                s5_scan(hf, q)()
        return step

    def c_proj(hf):
        def step():
            ybuf[:, hf * MXU_TILE:(hf + 1) * MXU_TILE] = jnp.dot(
                stbuf[hf], wc_ref[hf], preferred_element_type=f32)
        return step

    def s5_out():
        y = ybuf[...] + vec(VEC_D_BGLU, 0, W_S5) * ubuf[...]
        z = jax.nn.gelu(y)
        z = z * jax.nn.sigmoid(_bdot(z, wglu_ref[...]) + vec(VEC_D_BGLU, W_S5, W_S5))
        mixbuf[:, 0:W_S5] = _rms(z, vec(VEC_S5G_LRUG, 0, W_S5)).astype(bf16)

    def lru_conv():
        convbuf[...] = vec(VEC_CB_BA, 0, W_LRU) + sum(
            xrbuf[pl.ds(k * BATCH, ROWS), :] * vec(VEC_CW + k // 2, (k % 2) * W_LRU, W_LRU)
            for k in range(CONV_WIDTH))
        xrbuf[pl.ds(0, HIST_ROWS), :] = xrbuf[pl.ds(ROWS, HIST_ROWS), :]

    def lru_gates(hf):
        def step():
            c0 = hf * MXU_TILE
            cv = convbuf[:, c0:c0 + MXU_TILE]
            ri = _bdot(cv, wg_ref[hf])
            r = jax.nn.sigmoid(ri[:, 0:MXU_TILE] + vec(VEC_CB_BA, W_LRU + c0, MXU_TILE))
            ig = jax.nn.sigmoid(ri[:, MXU_TILE:] + vec(VEC_BX_CLOG, c0, MXU_TILE))
            log_a = vec(VEC_BX_CLOG, W_LRU + c0, MXU_TILE) * r
            a = jnp.exp(log_a)
            abuf[:, c0:c0 + MXU_TILE] = a
            one_minus_a2 = -jnp.tanh(log_a) * (a * a + 1.0)
            gbuf[:, c0:c0 + MXU_TILE] = jnp.sqrt(one_minus_a2) * (ig * cv)
        return step

    def lru_scan():
        hstate = lrustate[...]
        for t in range(TIME_CHUNK):
            rows = slice(t * BATCH, (t + 1) * BATCH)
            hstate = abuf[rows, :] * hstate + gbuf[rows, :]
            hlru[rows, :] = hstate
        lrustate[...] = hstate

    def lru_out():
        out = hlru[...] * jax.nn.gelu(yrbuf[...])
        mixbuf[:, W_S5:] = _rms(out, vec(VEC_S5G_LRUG, W_S5, W_LRU)).astype(bf16)

    def out_proj():
        xtb[...] = jnp.dot(mixbuf[...], wout_ref[...], preferred_element_type=f32)

    def scatter():
        for t in range(TIME_CHUNK):
            row = xtb[t * BATCH:(t + 1) * BATCH, :]
            for j in range(LANE_CHUNKS):
                osb[j, pl.ds(t, BATCH, stride=STAGE_SEG), :] = row[:, j * LANES:(j + 1) * LANES]

    def residual():
        for b in range(BATCH):
            delta = jnp.concatenate(
                [osb[j, b * STAGE_SEG:b * STAGE_SEG + TIME_CHUNK, :] for j in range(LANE_CHUNKS)],
                axis=-1)
            rows = slice(b * TIME_CHUNK, (b + 1) * TIME_CHUNK)
            o2d[rows, :] = x2d[rows, :] + delta

    return dict(
        norm_stage=norm_stage, in_proj=in_proj,
        b_proj0=b_proj(0), scan0=s5_scan_half(0), c_proj0=c_proj(0),
        b_proj1=b_proj(1), scan1=s5_scan_half(1), c_proj1=c_proj(1),
        s5_out=s5_out, lru_conv=lru_conv, lru_gates0=lru_gates(0), lru_gates1=lru_gates(1),
        lru_scan=lru_scan, lru_out=lru_out, out_proj=out_proj, scatter=scatter,
        residual=residual)


_MIXER_ORDER = (
    "norm_stage", "in_proj",
    "b_proj0", "lru_conv", "scan0",
    "c_proj0", "b_proj1", "lru_gates0", "scan1",
    "c_proj1", "lru_gates1", "s5_out", "lru_scan", "lru_out",
    "out_proj", "scatter", "residual",
)


def _mixer_kernel(x_ref, vec_ref, win_ref, wb_ref, wc_ref, wglu_ref, wg_ref, wout_ref,
                  o_ref,
                  xs, osb, xtb, hbuf, ubuf, xrbuf, yrbuf, bubuf, stbuf, ybuf, s5state,
                  convbuf, abuf, gbuf, hlru, lrustate, mixbuf):
    @pl.when(pl.program_id(0) == 0)
    def _():
        s5state[...] = jnp.zeros_like(s5state)
        lrustate[...] = jnp.zeros_like(lrustate)
        xrbuf[pl.ds(0, HIST_ROWS), :] = jnp.zeros((HIST_ROWS, W_LRU), jnp.float32)

    steps = _mixer_phases(
        x_ref.reshape(ROWS, D_MODEL), o_ref.reshape(ROWS, D_MODEL),
        vec_ref, win_ref, wb_ref, wc_ref, wglu_ref, wg_ref, wout_ref,
        xs, osb, xtb, hbuf, ubuf, xrbuf, yrbuf, bubuf, stbuf, ybuf, s5state,
        convbuf, abuf, gbuf, hlru, lrustate, mixbuf)
    assert sorted(steps) == sorted(_MIXER_ORDER)
    for name in _MIXER_ORDER:
        steps[name]()


def _ffn_kernel(x_ref, vec_ref, wgate_ref, wup_ref, wdown_ref, o_ref, actbuf):
    f32 = jnp.float32
    bf16 = jnp.bfloat16
    for part in range(FFN_PARTS):
        rows = slice(part * FFN_ROWS // FFN_PARTS, (part + 1) * FFN_ROWS // FFN_PARTS)
        x = x_ref[rows, :]
        h2 = _rms(x, vec_ref[0:1, :]).astype(bf16)
        for c in range(D_FF // FF_CHUNK):
            c0 = c * FF_CHUNK
            gate = jnp.dot(h2, wgate_ref[:, c0:c0 + FF_CHUNK], preferred_element_type=f32)
            up = jnp.dot(h2, wup_ref[:, c0:c0 + FF_CHUNK], preferred_element_type=f32)
            actbuf[rows, c0:c0 + FF_CHUNK] = (jax.nn.silu(gate) * up).astype(bf16)
        x2 = x + jnp.dot(actbuf[rows, :], wdown_ref[...], preferred_element_type=f32)
        o_ref[rows, :] = _rms(x2, vec_ref[1:2, :])


def _const_spec(shape):
    nd = len(shape)
    return pl.BlockSpec(shape, lambda i, _nd=nd: (0,) * _nd, pipeline_mode=pl.Buffered(1))


def _prepare_s5(lam_re, lam_im, log_step, b_re, b_im, c_re, c_im):
    f32 = jnp.float32
    la = jnp.minimum(lam_re.astype(f32), -1e-4)
    lb = lam_im.astype(f32)
    step = jnp.exp(log_step.astype(f32))[:, None]
    mag = jnp.exp(la * step)
    lbar_re = mag * jnp.cos(lb * step)
    lbar_im = mag * jnp.sin(lb * step)
    den = la * la + lb * lb
    q_re = (((lbar_re - 1.0) * la + lbar_im * lb) / den)[..., None]
    q_im = ((lbar_im * la - (lbar_re - 1.0) * lb) / den)[..., None]
    bre = b_re.astype(f32)
    bim = b_im.astype(f32)
    bbar = jnp.stack([q_re * bre - q_im * bim, q_re * bim + q_im * bre])
    cc = jnp.stack([c_re.astype(f32), -c_im.astype(f32)])
    groups = jnp.arange(HALF_GROUPS)[:, None]
    xb = bbar.reshape(2, S5_HALVES, HALF_GROUPS, S5_STATE, S5_GROUP_CH).transpose(1, 2, 4, 0, 3)
    on_diag = jnp.arange(HALF_STATES)[None, :] // S5_STATE == groups
    wb = jnp.where(on_diag[None, :, None, None, :], jnp.tile(xb, (1, 1, 1, 1, HALF_GROUPS)), 0.0)
    wb = wb.reshape(S5_HALVES, MXU_TILE, 2 * HALF_STATES)
    xc = cc.reshape(2, S5_HALVES, HALF_GROUPS, S5_GROUP_CH, S5_STATE).transpose(1, 0, 2, 4, 3)
    on_diag = jnp.arange(MXU_TILE)[None, :] // S5_GROUP_CH == groups
    wc = jnp.where(on_diag[None, None, :, None, :], jnp.tile(xc, (1, 1, 1, 1, HALF_GROUPS)), 0.0)
    wc = wc.reshape(S5_HALVES, 2 * HALF_STATES, MXU_TILE)
    lam_rows = jnp.stack([lbar_re.reshape(S5_HALVES, HALF_STATES),
                          lbar_im.reshape(S5_HALVES, HALF_STATES)], axis=1)
    return wb.astype(jnp.bfloat16), wc.astype(jnp.bfloat16), lam_rows


def _prepare_lru_gates(w_a, w_x):
    f32 = jnp.float32
    ww = jnp.stack([w_a.astype(f32), w_x.astype(f32)]).reshape(
        2, 2, LRU_HALF_HEADS, LRU_HEAD_DIM, LRU_HEAD_DIM).transpose(1, 2, 3, 0, 4)
    on_diag = (jnp.arange(MXU_TILE)[None, :] // LRU_HEAD_DIM
               == jnp.arange(LRU_HALF_HEADS)[:, None])
    wg = jnp.where(on_diag[None, :, None, None, :], jnp.tile(ww, (1, 1, 1, 1, LRU_HALF_HEADS)), 0.0)
    return wg.reshape(2, MXU_TILE, 2 * MXU_TILE).astype(jnp.bfloat16)


def _mixer(x, norm1_g, w_in, s5_lambda_re, s5_lambda_im, s5_log_step, s5_b_re, s5_b_im,
           s5_c_re, s5_c_im, s5_d, s5_w_glu, s5_b_glu, lru_conv_w, lru_conv_b, lru_w_a, lru_b_a,
           lru_w_x, lru_b_x, lru_lambda, s5_out_g, lru_out_g, w_out):
    f32 = jnp.float32
    bf16 = jnp.bfloat16
    wb, wc, lam_rows = _prepare_s5(s5_lambda_re, s5_lambda_im, s5_log_step, s5_b_re, s5_b_im,
                                   s5_c_re, s5_c_im)
    wg = _prepare_lru_gates(lru_w_a, lru_w_x)
    clog = RG_C * jax.nn.log_sigmoid(lru_lambda.astype(f32))
    used = [norm1_g, s5_d, s5_b_glu, lru_conv_b, lru_b_a, lru_b_x, clog, s5_out_g, lru_out_g,
            lru_conv_w, lam_rows]
    flat = [v.astype(f32).reshape(-1) for v in used]
    pad = VEC_ROWS * D_MODEL - sum(v.size for v in flat)
    vecs = jnp.concatenate(flat + [jnp.zeros((pad,), f32)]).reshape(VEC_ROWS, D_MODEL)

    inputs = [x, vecs, w_in.astype(bf16), wb, wc, s5_w_glu.astype(bf16), wg, w_out.astype(bf16)]
    x_spec = pl.BlockSpec((BATCH, TIME_CHUNK, D_MODEL), lambda i: (0, i, 0))
    in_specs = [x_spec] + [_const_spec(a.shape) for a in inputs[1:]]
    scratch = [
        pltpu.VMEM((LANE_CHUNKS, STAGE_ROWS, LANES), f32),
        pltpu.VMEM((LANE_CHUNKS, STAGE_ROWS, LANES), f32),
        pltpu.VMEM((ROWS, D_MODEL), f32),
        pltpu.VMEM((ROWS, D_MODEL), bf16),
        pltpu.VMEM((ROWS, W_S5), f32),
        pltpu.VMEM((ROWS + HIST_ROWS, W_LRU), f32),
        pltpu.VMEM((ROWS, W_LRU), f32),
        pltpu.VMEM((S5_HALVES, ROWS, 2 * HALF_STATES), f32),
        pltpu.VMEM((S5_HALVES, ROWS, 2 * HALF_STATES), bf16),
        pltpu.VMEM((ROWS, W_S5), f32),
        pltpu.VMEM((BATCH, 2 * S5_GROUPS * S5_STATE), f32),
        pltpu.VMEM((ROWS, W_LRU), f32),
        pltpu.VMEM((ROWS, W_LRU), f32),
        pltpu.VMEM((ROWS, W_LRU), f32),
        pltpu.VMEM((ROWS, W_LRU), f32),
        pltpu.VMEM((BATCH, W_LRU), f32),
        pltpu.VMEM((ROWS, D_MODEL), bf16),
    ]
    return pl.pallas_call(
        _mixer_kernel,
        grid=(NUM_CHUNKS,),
        in_specs=in_specs,
        out_specs=x_spec,
        out_shape=jax.ShapeDtypeStruct((BATCH, SEQ, D_MODEL), f32),
        scratch_shapes=scratch,
        compiler_params=pltpu.CompilerParams(
            dimension_semantics=("arbitrary",), vmem_limit_bytes=VMEM_LIMIT_BYTES),
        name="mixer",
    )(*inputs)


def _ffn(x1, norm2_g, w_gate, w_up, w_down, final_g):
    f32 = jnp.float32
    bf16 = jnp.bfloat16
    rows = x1.shape[0]
    row_spec = pl.BlockSpec((FFN_ROWS, D_MODEL), lambda i: (i, 0))
    vecs = jnp.concatenate([norm2_g.astype(f32), final_g.astype(f32),
                            jnp.zeros(((SUBLANES - 2) * D_MODEL,), f32)]).reshape(SUBLANES, D_MODEL)
    inputs = [x1, vecs, w_gate.astype(bf16), w_up.astype(bf16), w_down.astype(bf16)]
    return pl.pallas_call(
        _ffn_kernel,
        grid=(rows // FFN_ROWS,),
        in_specs=[row_spec] + [_const_spec(a.shape) for a in inputs[1:]],
        out_specs=row_spec,
        out_shape=jax.ShapeDtypeStruct((rows, D_MODEL), f32),
        scratch_shapes=[pltpu.VMEM((FFN_ROWS, D_FF), bf16)],
        compiler_params=pltpu.CompilerParams(
            dimension_semantics=("arbitrary",), vmem_limit_bytes=VMEM_LIMIT_BYTES),
        name="ffn",
    )(*inputs)


def kernel(x, norm1_g, w_in, s5_lambda_re, s5_lambda_im, s5_log_step, s5_b_re, s5_b_im, s5_c_re, s5_c_im, s5_d, s5_w_glu, s5_b_glu, lru_conv_w, lru_conv_b, lru_w_a, lru_b_a, lru_w_x, lru_b_x, lru_lambda, s5_out_g, lru_out_g, w_out, norm2_g, w_gate, w_up, w_down, final_g):
    x1 = _mixer(
        x, norm1_g[0], w_in[0], s5_lambda_re[0], s5_lambda_im[0], s5_log_step[0], s5_b_re[0],
        s5_b_im[0], s5_c_re[0], s5_c_im[0], s5_d[0], s5_w_glu[0], s5_b_glu[0], lru_conv_w[0],
        lru_conv_b[0], lru_w_a[0], lru_b_a[0], lru_w_x[0], lru_b_x[0], lru_lambda[0],
        s5_out_g[0], lru_out_g[0], w_out[0])
    out = _ffn(x1.reshape(BATCH * SEQ, D_MODEL), norm2_g[0], w_gate[0], w_up[0], w_down[0],
               final_g)
    return out.reshape(BATCH, SEQ, D_MODEL)
```

```python
import jax
import jax.numpy as jnp
from jax import lax
from jax.experimental import pallas as pl
from jax.experimental.pallas import tpu as pltpu

D_MODEL = 1024
BATCH = 16
SEQ = 2048
W_S5 = 512
S5_GROUP_CH = 16
S5_GROUPS = 32
S5_STATE = 64
W_LRU = 512
LRU_HEADS = 8
LRU_HEAD_DIM = 64
CONV_WIDTH = 4
RG_C = 8.0
D_FF = 2816
EPS = 1e-6

MXU_TILE = 256
S5_HALVES = W_S5 // MXU_TILE
HALF_GROUPS = S5_GROUPS // S5_HALVES
HALF_STATES = HALF_GROUPS * S5_STATE
SCAN_SLAB = 512
LRU_HALF_HEADS = LRU_HEADS // 2

TIME_CHUNK = 32
NUM_CHUNKS = SEQ // TIME_CHUNK
ROWS = TIME_CHUNK * BATCH
HIST_ROWS = (CONV_WIDTH - 1) * BATCH
LANES = 128
LANE_CHUNKS = D_MODEL // LANES
SUBLANES = 8
STAGE_SEG = TIME_CHUNK + SUBLANES
STAGE_ROWS = BATCH * STAGE_SEG
TIME_SLICES = 2
SLICE_STEPS = TIME_CHUNK // TIME_SLICES
FF_CHUNK = 256
FFN_ROWS = 1024
FFN_PARTS = 4
VMEM_LIMIT_BYTES = 56 * 1024 * 1024
VEC_G1 = 0
VEC_D_BGLU = 1
VEC_CB_BA = 2
VEC_BX_CLOG = 3
VEC_S5G_LRUG = 4
VEC_CW = 5
VEC_LAM = 7
VEC_ROWS = 16


def _rms(x, g):
    ms = jnp.mean(x * x, axis=-1, keepdims=True)
    return x * lax.rsqrt(ms + EPS) * g


def _bdot(a, b):
    return jnp.dot(a.astype(jnp.bfloat16), b, preferred_element_type=jnp.float32)


def _mixer_phases(x2d, o2d, vec_ref, win_ref, wb_ref, wc_ref, wglu_ref, wg_ref,
                  wout_ref, xs, osb, xtb, hbuf, ubuf, xrbuf, yrbuf, bubuf, stbuf, ybuf, s5state,
                  convbuf, abuf, gbuf, hlru, lrustate, mixbuf):
    f32 = jnp.float32
    bf16 = jnp.bfloat16
    slice_rows = SLICE_STEPS * BATCH

    def vec(row, col0, width):
        return vec_ref[row:row + 1, col0:col0 + width]

    def rows_of(p):
        return slice(p * slice_rows, (p + 1) * slice_rows)

    def norm_stage(p):
        t0 = p * SLICE_STEPS
        g1 = vec(VEC_G1, 0, D_MODEL)
        for b in range(BATCH):
            hn = _rms(x2d[b * TIME_CHUNK + t0:b * TIME_CHUNK + t0 + SLICE_STEPS, :], g1)
            for j in range(LANE_CHUNKS):
                xs[j, b * STAGE_SEG + t0:b * STAGE_SEG + t0 + SLICE_STEPS, :] = (
                    hn[:, j * LANES:(j + 1) * LANES])
        for t in range(t0, t0 + SLICE_STEPS):
            ht = jnp.concatenate(
                [xs[j, pl.ds(t, BATCH, stride=STAGE_SEG), :] for j in range(LANE_CHUNKS)],
                axis=-1)
            hbuf[t * BATCH:(t + 1) * BATCH, :] = ht.astype(bf16)

    def in_proj(p):
        rows = rows_of(p)
        h = hbuf[rows, :]
        ubuf[rows, :] = jnp.dot(h, win_ref[:, 0:W_S5], preferred_element_type=f32)
        xrbuf[pl.ds(HIST_ROWS + p * slice_rows, slice_rows), :] = jnp.dot(
            h, win_ref[:, W_S5:W_S5 + W_LRU], preferred_element_type=f32)
        yrbuf[rows, :] = jnp.dot(h, win_ref[:, W_S5 + W_LRU:], preferred_element_type=f32)

    def b_proj(p, hf):
        rows = rows_of(p)
        bubuf[hf, rows, :] = _bdot(ubuf[rows, hf * MXU_TILE:(hf + 1) * MXU_TILE], wb_ref[hf])

    def s5_scan(p, hf):
        for q in range(HALF_STATES // SCAN_SLAB):
            re0 = q * SCAN_SLAB
            im0 = HALF_STATES + q * SCAN_SLAB
            sbase = hf * 2 * HALF_STATES
            lr = jnp.broadcast_to(vec(VEC_LAM + 2 * hf, re0, SCAN_SLAB), (BATCH, SCAN_SLAB))
            li = jnp.broadcast_to(vec(VEC_LAM + 2 * hf + 1, re0, SCAN_SLAB), (BATCH, SCAN_SLAB))
            sre = s5state[:, sbase + re0:sbase + re0 + SCAN_SLAB]
            sim = s5state[:, sbase + im0:sbase + im0 + SCAN_SLAB]
            for t in range(p * SLICE_STEPS, (p + 1) * SLICE_STEPS):
                rows = slice(t * BATCH, (t + 1) * BATCH)
                bre = bubuf[hf, rows, re0:re0 + SCAN_SLAB]
                bim = bubuf[hf, rows, im0:im0 + SCAN_SLAB]
                sre, sim = lr * sre - li * sim + bre, lr * sim + li * sre + bim
                stbuf[hf, rows, re0:re0 + SCAN_SLAB] = sre.astype(bf16)
                stbuf[hf, rows, im0:im0 + SCAN_SLAB] = sim.astype(bf16)
            s5state[:, sbase + re0:sbase + re0 + SCAN_SLAB] = sre
            s5state[:, sbase + im0:sbase + im0 + SCAN_SLAB] = sim

    def c_proj(p, hf):
        rows = rows_of(p)
        ybuf[rows, hf * MXU_TILE:(hf + 1) * MXU_TILE] = jnp.dot(
            stbuf[hf, rows, :], wc_ref[hf], preferred_element_type=f32)

    def s5_out(p):
        rows = rows_of(p)
        y = ybuf[rows, :] + vec(VEC_D_BGLU, 0, W_S5) * ubuf[rows, :]
        z = jax.nn.gelu(y)
        z = z * jax.nn.sigmoid(_bdot(z, wglu_ref[...]) + vec(VEC_D_BGLU, W_S5, W_S5))
        mixbuf[rows, 0:W_S5] = _rms(z, vec(VEC_S5G_LRUG, 0, W_S5)).astype(bf16)

    def lru_conv(p):
        convbuf[rows_of(p), :] = vec(VEC_CB_BA, 0, W_LRU) + sum(
            xrbuf[pl.ds(k * BATCH + p * slice_rows, slice_rows), :]
            * vec(VEC_CW + k // 2, (k % 2) * W_LRU, W_LRU)
            for k in range(CONV_WIDTH))
        if p == TIME_SLICES - 1:
            xrbuf[pl.ds(0, HIST_ROWS), :] = xrbuf[pl.ds(ROWS, HIST_ROWS), :]

    def lru_gates(p, hf):
        rows = rows_of(p)
        c0 = hf * MXU_TILE
        cv = convbuf[rows, c0:c0 + MXU_TILE]
        ri = _bdot(cv, wg_ref[hf])
        r = jax.nn.sigmoid(ri[:, 0:MXU_TILE] + vec(VEC_CB_BA, W_LRU + c0, MXU_TILE))
        ig = jax.nn.sigmoid(ri[:, MXU_TILE:] + vec(VEC_BX_CLOG, c0, MXU_TILE))
        log_a = vec(VEC_BX_CLOG, W_LRU + c0, MXU_TILE) * r
        a = jnp.exp(log_a)
        abuf[rows, c0:c0 + MXU_TILE] = a
        one_minus_a2 = -jnp.tanh(log_a) * (a * a + 1.0)
        gbuf[rows, c0:c0 + MXU_TILE] = jnp.sqrt(one_minus_a2) * (ig * cv)

    def lru_scan(p):
        hstate = lrustate[...]
        for t in range(p * SLICE_STEPS, (p + 1) * SLICE_STEPS):
            rows = slice(t * BATCH, (t + 1) * BATCH)
            hstate = abuf[rows, :] * hstate + gbuf[rows, :]
            hlru[rows, :] = hstate
        lrustate[...] = hstate

    def lru_out(p):
        rows = rows_of(p)
        out = hlru[rows, :] * jax.nn.gelu(yrbuf[rows, :])
        mixbuf[rows, W_S5:] = _rms(out, vec(VEC_S5G_LRUG, W_S5, W_LRU)).astype(bf16)

    def out_proj(p):
        rows = rows_of(p)
        xtb[rows, :] = jnp.dot(mixbuf[rows, :], wout_ref[...], preferred_element_type=f32)

    def scatter(p):
        t0 = p * SLICE_STEPS
        for t in range(t0, t0 + SLICE_STEPS):
            row = xtb[t * BATCH:(t + 1) * BATCH, :]
            for j in range(LANE_CHUNKS):
                osb[j, pl.ds(t, BATCH, stride=STAGE_SEG), :] = row[:, j * LANES:(j + 1) * LANES]
        for b in range(BATCH):
            seg = slice(b * STAGE_SEG + t0, b * STAGE_SEG + t0 + SLICE_STEPS)
            delta = jnp.concatenate([osb[j, seg, :] for j in range(LANE_CHUNKS)], axis=-1)
            rows = slice(b * TIME_CHUNK + t0, b * TIME_CHUNK + t0 + SLICE_STEPS)
            o2d[rows, :] = x2d[rows, :] + delta

    per_slice = dict(norm_stage=norm_stage, in_proj=in_proj, s5_out=s5_out, lru_conv=lru_conv,
                     lru_scan=lru_scan, lru_out=lru_out, out_proj=out_proj, scatter=scatter)
    per_half = dict(b_proj=b_proj, s5_scan=s5_scan, c_proj=c_proj, lru_gates=lru_gates)
    steps = {(name, p): (lambda f=f, p=p: f(p))
             for name, f in per_slice.items() for p in range(TIME_SLICES)}
    steps.update({(name, p, hf): (lambda f=f, p=p, hf=hf: f(p, hf))
                  for name, f in per_half.items() for p in range(TIME_SLICES)
                  for hf in range(S5_HALVES)})
    return steps


def _slice_order(p):
    return (
        ("norm_stage", p), ("in_proj", p),
        ("b_proj", p, 0), ("lru_conv", p), ("s5_scan", p, 0),
        ("c_proj", p, 0), ("b_proj", p, 1), ("lru_gates", p, 0), ("s5_scan", p, 1),
        ("c_proj", p, 1), ("lru_gates", p, 1), ("s5_out", p), ("lru_scan", p), ("lru_out", p),
        ("out_proj", p), ("scatter", p),
    )


def _interleaved_order(lag):
    first, second = _slice_order(0), _slice_order(1)
    order = list(first[:lag])
    for k in range(len(second)):
        if lag + k < len(first):
            order.append(first[lag + k])
        order.append(second[k])
    return tuple(order)


assert TIME_SLICES == 2
_MIXER_ORDER = _interleaved_order(lag=3)


def _mixer_kernel(x_ref, vec_ref, win_ref, wb_ref, wc_ref, wglu_ref, wg_ref, wout_ref,
                  o_ref,
                  xs, osb, xtb, hbuf, ubuf, xrbuf, yrbuf, bubuf, stbuf, ybuf, s5state,
                  convbuf, abuf, gbuf, hlru, lrustate, mixbuf):
    @pl.when(pl.program_id(0) == 0)
    def _():
        s5state[...] = jnp.zeros_like(s5state)
        lrustate[...] = jnp.zeros_like(lrustate)
        xrbuf[pl.ds(0, HIST_ROWS), :] = jnp.zeros((HIST_ROWS, W_LRU), jnp.float32)

    steps = _mixer_phases(
        x_ref.reshape(ROWS, D_MODEL), o_ref.reshape(ROWS, D_MODEL),
        vec_ref, win_ref, wb_ref, wc_ref, wglu_ref, wg_ref, wout_ref,
        xs, osb, xtb, hbuf, ubuf, xrbuf, yrbuf, bubuf, stbuf, ybuf, s5state,
        convbuf, abuf, gbuf, hlru, lrustate, mixbuf)
    assert sorted(steps) == sorted(_MIXER_ORDER)
    for name in _MIXER_ORDER:
        steps[name]()


def _ffn_kernel(x_ref, vec_ref, wgate_ref, wup_ref, wdown_ref, o_ref, actbuf):
    f32 = jnp.float32
    bf16 = jnp.bfloat16
    for part in range(FFN_PARTS):
        rows = slice(part * FFN_ROWS // FFN_PARTS, (part + 1) * FFN_ROWS // FFN_PARTS)
        x = x_ref[rows, :]
        h2 = _rms(x, vec_ref[0:1, :]).astype(bf16)
        for c in range(D_FF // FF_CHUNK):
            c0 = c * FF_CHUNK
            gate = jnp.dot(h2, wgate_ref[:, c0:c0 + FF_CHUNK], preferred_element_type=f32)
            up = jnp.dot(h2, wup_ref[:, c0:c0 + FF_CHUNK], preferred_element_type=f32)
            actbuf[rows, c0:c0 + FF_CHUNK] = (jax.nn.silu(gate) * up).astype(bf16)
        x2 = x + jnp.dot(actbuf[rows, :], wdown_ref[...], preferred_element_type=f32)
        o_ref[rows, :] = _rms(x2, vec_ref[1:2, :])


def _const_spec(shape):
    nd = len(shape)
    return pl.BlockSpec(shape, lambda i, _nd=nd: (0,) * _nd, pipeline_mode=pl.Buffered(1))


def _prepare_s5(lam_re, lam_im, log_step, b_re, b_im, c_re, c_im):
    f32 = jnp.float32
    la = jnp.minimum(lam_re.astype(f32), -1e-4)
    lb = lam_im.astype(f32)
    step = jnp.exp(log_step.astype(f32))[:, None]
    mag = jnp.exp(la * step)
    lbar_re = mag * jnp.cos(lb * step)
    lbar_im = mag * jnp.sin(lb * step)
    den = la * la + lb * lb
    q_re = (((lbar_re - 1.0) * la + lbar_im * lb) / den)[..., None]
    q_im = ((lbar_im * la - (lbar_re - 1.0) * lb) / den)[..., None]
    bre = b_re.astype(f32)
    bim = b_im.astype(f32)
    bbar = jnp.stack([q_re * bre - q_im * bim, q_re * bim + q_im * bre])
    cc = jnp.stack([c_re.astype(f32), -c_im.astype(f32)])
    groups = jnp.arange(HALF_GROUPS)[:, None]
    xb = bbar.reshape(2, S5_HALVES, HALF_GROUPS, S5_STATE, S5_GROUP_CH).transpose(1, 2, 4, 0, 3)
    on_diag = jnp.arange(HALF_STATES)[None, :] // S5_STATE == groups
    wb = jnp.where(on_diag[None, :, None, None, :], jnp.tile(xb, (1, 1, 1, 1, HALF_GROUPS)), 0.0)
    wb = wb.reshape(S5_HALVES, MXU_TILE, 2 * HALF_STATES)
    xc = cc.reshape(2, S5_HALVES, HALF_GROUPS, S5_GROUP_CH, S5_STATE).transpose(1, 0, 2, 4, 3)
    on_diag = jnp.arange(MXU_TILE)[None, :] // S5_GROUP_CH == groups
    wc = jnp.where(on_diag[None, None, :, None, :], jnp.tile(xc, (1, 1, 1, 1, HALF_GROUPS)), 0.0)
    wc = wc.reshape(S5_HALVES, 2 * HALF_STATES, MXU_TILE)
    lam_rows = jnp.stack([lbar_re.reshape(S5_HALVES, HALF_STATES),
                          lbar_im.reshape(S5_HALVES, HALF_STATES)], axis=1)
    return wb.astype(jnp.bfloat16), wc.astype(jnp.bfloat16), lam_rows


def _prepare_lru_gates(w_a, w_x):
    f32 = jnp.float32
    ww = jnp.stack([w_a.astype(f32), w_x.astype(f32)]).reshape(
        2, 2, LRU_HALF_HEADS, LRU_HEAD_DIM, LRU_HEAD_DIM).transpose(1, 2, 3, 0, 4)
    on_diag = (jnp.arange(MXU_TILE)[None, :] // LRU_HEAD_DIM
               == jnp.arange(LRU_HALF_HEADS)[:, None])
    wg = jnp.where(on_diag[None, :, None, None, :], jnp.tile(ww, (1, 1, 1, 1, LRU_HALF_HEADS)), 0.0)
    return wg.reshape(2, MXU_TILE, 2 * MXU_TILE).astype(jnp.bfloat16)


def _mixer(x, norm1_g, w_in, s5_lambda_re, s5_lambda_im, s5_log_step, s5_b_re, s5_b_im,
           s5_c_re, s5_c_im, s5_d, s5_w_glu, s5_b_glu, lru_conv_w, lru_conv_b, lru_w_a, lru_b_a,
           lru_w_x, lru_b_x, lru_lambda, s5_out_g, lru_out_g, w_out):
    f32 = jnp.float32
    bf16 = jnp.bfloat16
    wb, wc, lam_rows = _prepare_s5(s5_lambda_re, s5_lambda_im, s5_log_step, s5_b_re, s5_b_im,
                                   s5_c_re, s5_c_im)
    wg = _prepare_lru_gates(lru_w_a, lru_w_x)
    clog = RG_C * jax.nn.log_sigmoid(lru_lambda.astype(f32))
    used = [norm1_g, s5_d, s5_b_glu, lru_conv_b, lru_b_a, lru_b_x, clog, s5_out_g, lru_out_g,
            lru_conv_w, lam_rows]
    flat = [v.astype(f32).reshape(-1) for v in used]
    pad = VEC_ROWS * D_MODEL - sum(v.size for v in flat)
    vecs = jnp.concatenate(flat + [jnp.zeros((pad,), f32)]).reshape(VEC_ROWS, D_MODEL)

    inputs = [x, vecs, w_in.astype(bf16), wb, wc, s5_w_glu.astype(bf16), wg, w_out.astype(bf16)]
    x_spec = pl.BlockSpec((BATCH, TIME_CHUNK, D_MODEL), lambda i: (0, i, 0))
    in_specs = [x_spec] + [_const_spec(a.shape) for a in inputs[1:]]
    scratch = [
        pltpu.VMEM((LANE_CHUNKS, STAGE_ROWS, LANES), f32),
        pltpu.VMEM((LANE_CHUNKS, STAGE_ROWS, LANES), f32),
        pltpu.VMEM((ROWS, D_MODEL), f32),
        pltpu.VMEM((ROWS, D_MODEL), bf16),
        pltpu.VMEM((ROWS, W_S5), f32),
        pltpu.VMEM((ROWS + HIST_ROWS, W_LRU), f32),
        pltpu.VMEM((ROWS, W_LRU), f32),
        pltpu.VMEM((S5_HALVES, ROWS, 2 * HALF_STATES), f32),
        pltpu.VMEM((S5_HALVES, ROWS, 2 * HALF_STATES), bf16),
        pltpu.VMEM((ROWS, W_S5), f32),
        pltpu.VMEM((BATCH, 2 * S5_GROUPS * S5_STATE), f32),
        pltpu.VMEM((ROWS, W_LRU), f32),
        pltpu.VMEM((ROWS, W_LRU), f32),
        pltpu.VMEM((ROWS, W_LRU), f32),
        pltpu.VMEM((ROWS, W_LRU), f32),
        pltpu.VMEM((BATCH, W_LRU), f32),
        pltpu.VMEM((ROWS, D_MODEL), bf16),
    ]
    return pl.pallas_call(
        _mixer_kernel,
        grid=(NUM_CHUNKS,),
        in_specs=in_specs,
        out_specs=x_spec,
        out_shape=jax.ShapeDtypeStruct((BATCH, SEQ, D_MODEL), f32),
        scratch_shapes=scratch,
        compiler_params=pltpu.CompilerParams(
            dimension_semantics=("arbitrary",), vmem_limit_bytes=VMEM_LIMIT_BYTES),
        name="mixer",
    )(*inputs)


def _ffn(x1, norm2_g, w_gate, w_up, w_down, final_g):
    f32 = jnp.float32
    bf16 = jnp.bfloat16
    rows = x1.shape[0]
    row_spec = pl.BlockSpec((FFN_ROWS, D_MODEL), lambda i: (i, 0))
    vecs = jnp.concatenate([norm2_g.astype(f32), final_g.astype(f32),
                            jnp.zeros(((SUBLANES - 2) * D_MODEL,), f32)]).reshape(SUBLANES, D_MODEL)
    inputs = [x1, vecs, w_gate.astype(bf16), w_up.astype(bf16), w_down.astype(bf16)]
    return pl.pallas_call(
        _ffn_kernel,
        grid=(rows // FFN_ROWS,),
        in_specs=[row_spec] + [_const_spec(a.shape) for a in inputs[1:]],
        out_specs=row_spec,
        out_shape=jax.ShapeDtypeStruct((rows, D_MODEL), f32),
        scratch_shapes=[pltpu.VMEM((FFN_ROWS, D_FF), bf16)],
        compiler_params=pltpu.CompilerParams(
            dimension_semantics=("arbitrary",), vmem_limit_bytes=VMEM_LIMIT_BYTES),
        name="ffn",
    )(*inputs)


def kernel(x, norm1_g, w_in, s5_lambda_re, s5_lambda_im, s5_log_step, s5_b_re, s5_b_im, s5_c_re, s5_c_im, s5_d, s5_w_glu, s5_b_glu, lru_conv_w, lru_conv_b, lru_w_a, lru_b_a, lru_w_x, lru_b_x, lru_lambda, s5_out_g, lru_out_g, w_out, norm2_g, w_gate, w_up, w_down, final_g):
    x1 = _mixer(
        x, norm1_g[0], w_in[0], s5_lambda_re[0], s5_lambda_im[0], s5_log_step[0], s5_b_re[0],
        s5_b_im[0], s5_c_re[0], s5_c_im[0], s5_d[0], s5_w_glu[0], s5_b_glu[0], lru_conv_w[0],
        lru_conv_b[0], lru_w_a[0], lru_b_a[0], lru_w_x[0], lru_b_x[0], lru_lambda[0],
        s5_out_g[0], lru_out_g[0], w_out[0])
    out = _ffn(x1.reshape(BATCH * SEQ, D_MODEL), norm2_g[0], w_gate[0], w_up[0], w_down[0],
               final_g)
    return out.reshape(BATCH, SEQ, D_MODEL)
```

```python
import jax
import jax.numpy as jnp
from jax import lax
from jax.experimental import pallas as pl
from jax.experimental.pallas import tpu as pltpu

D_MODEL = 1024
BATCH = 16
SEQ = 2048
W_S5 = 512
S5_GROUP_CH = 16
S5_GROUPS = 32
S5_STATE = 64
W_LRU = 512
LRU_HEADS = 8
LRU_HEAD_DIM = 64
CONV_WIDTH = 4
RG_C = 8.0
D_FF = 2816
EPS = 1e-6

MXU_TILE = 256
S5_HALVES = W_S5 // MXU_TILE
HALF_GROUPS = S5_GROUPS // S5_HALVES
HALF_STATES = HALF_GROUPS * S5_STATE
SCAN_SLAB = 512
LRU_HALF_HEADS = LRU_HEADS // 2

TIME_CHUNK = 32
NUM_CHUNKS = SEQ // TIME_CHUNK
ROWS = TIME_CHUNK * BATCH
HIST_ROWS = (CONV_WIDTH - 1) * BATCH
LANES = 128
LANE_CHUNKS = D_MODEL // LANES
SUBLANES = 8
STAGE_SEG = TIME_CHUNK + SUBLANES
STAGE_ROWS = BATCH * STAGE_SEG
TIME_SLICES = 2
SLICE_STEPS = TIME_CHUNK // TIME_SLICES
FF_CHUNK = 256
FFN_ROWS = 1024
FFN_PARTS = 4
VMEM_LIMIT_BYTES = 56 * 1024 * 1024
VEC_G1 = 0
VEC_D_BGLU = 1
VEC_CB_BA = 2
VEC_BX_CLOG = 3
VEC_S5G_LRUG = 4
VEC_CW = 5
VEC_LAM = 7
VEC_ROWS = 16


def _rms(x, g):
    ms = jnp.mean(x * x, axis=-1, keepdims=True)
    return x * lax.rsqrt(ms + EPS) * g


def _bdot(a, b):
    return jnp.dot(a.astype(jnp.bfloat16), b, preferred_element_type=jnp.float32)


def _mixer_phases(x2d, o2d, vec_ref, win_ref, wb_ref, wc_ref, wglu_ref, wg_ref,
                  wout_ref, xs, osb, xtb, hbuf, ubuf, xrbuf, yrbuf, bubuf, stbuf, ybuf, s5state,
                  convbuf, abuf, gbuf, hlru, lrustate, mixbuf):
    f32 = jnp.float32
    bf16 = jnp.bfloat16
    slice_rows = SLICE_STEPS * BATCH

    def vec(row, col0, width):
        return vec_ref[row:row + 1, col0:col0 + width]

    def rows_of(p):
        return slice(p * slice_rows, (p + 1) * slice_rows)

    def norm_stage(p):
        t0 = p * SLICE_STEPS
        g1 = vec(VEC_G1, 0, D_MODEL)
        for b in range(BATCH):
            hn = _rms(x2d[b * TIME_CHUNK + t0:b * TIME_CHUNK + t0 + SLICE_STEPS, :], g1)
            for j in range(LANE_CHUNKS):
                xs[j, b * STAGE_SEG + t0:b * STAGE_SEG + t0 + SLICE_STEPS, :] = (
                    hn[:, j * LANES:(j + 1) * LANES])
        for t in range(t0, t0 + SLICE_STEPS):
            ht = jnp.concatenate(
                [xs[j, pl.ds(t, BATCH, stride=STAGE_SEG), :] for j in range(LANE_CHUNKS)],
                axis=-1)
            hbuf[t * BATCH:(t + 1) * BATCH, :] = ht.astype(bf16)

    def in_proj(p):
        rows = rows_of(p)
        h = hbuf[rows, :]
        ubuf[rows, :] = jnp.dot(h, win_ref[:, 0:W_S5], preferred_element_type=f32)
        xrbuf[pl.ds(HIST_ROWS + p * slice_rows, slice_rows), :] = jnp.dot(
            h, win_ref[:, W_S5:W_S5 + W_LRU], preferred_element_type=f32)
        yrbuf[rows, :] = jnp.dot(h, win_ref[:, W_S5 + W_LRU:], preferred_element_type=f32)

    def b_proj(p, hf):
        rows = rows_of(p)
        bubuf[hf, rows, :] = _bdot(ubuf[rows, hf * MXU_TILE:(hf + 1) * MXU_TILE], wb_ref[hf])

    def s5_scan(p, hf):
        for q in range(HALF_STATES // SCAN_SLAB):
            re0 = q * SCAN_SLAB
            im0 = HALF_STATES + q * SCAN_SLAB
            sbase = hf * 2 * HALF_STATES
            lr = jnp.broadcast_to(vec(VEC_LAM + 2 * hf, re0, SCAN_SLAB), (BATCH, SCAN_SLAB))
            li = jnp.broadcast_to(vec(VEC_LAM + 2 * hf + 1, re0, SCAN_SLAB), (BATCH, SCAN_SLAB))
            sre = s5state[:, sbase + re0:sbase + re0 + SCAN_SLAB]
            sim = s5state[:, sbase + im0:sbase + im0 + SCAN_SLAB]
            for t in range(p * SLICE_STEPS, (p + 1) * SLICE_STEPS):
                rows = slice(t * BATCH, (t + 1) * BATCH)
                bre = bubuf[hf, rows, re0:re0 + SCAN_SLAB]
                bim = bubuf[hf, rows, im0:im0 + SCAN_SLAB]
                sre, sim = lr * sre - li * sim + bre, lr * sim + li * sre + bim
                stbuf[hf, rows, re0:re0 + SCAN_SLAB] = sre.astype(bf16)
                stbuf[hf, rows, im0:im0 + SCAN_SLAB] = sim.astype(bf16)
            s5state[:, sbase + re0:sbase + re0 + SCAN_SLAB] = sre
            s5state[:, sbase + im0:sbase + im0 + SCAN_SLAB] = sim

    def c_proj(p, hf):
        rows = rows_of(p)
        ybuf[rows, hf * MXU_TILE:(hf + 1) * MXU_TILE] = jnp.dot(
            stbuf[hf, rows, :], wc_ref[hf], preferred_element_type=f32)

    def s5_out(p):
        rows = rows_of(p)
        y = ybuf[rows, :] + vec(VEC_D_BGLU, 0, W_S5) * ubuf[rows, :]
        z = jax.nn.gelu(y)
        z = z * jax.nn.sigmoid(_bdot(z, wglu_ref[...]) + vec(VEC_D_BGLU, W_S5, W_S5))
        mixbuf[rows, 0:W_S5] = _rms(z, vec(VEC_S5G_LRUG, 0, W_S5)).astype(bf16)

    def lru_conv(p):
        convbuf[rows_of(p), :] = vec(VEC_CB_BA, 0, W_LRU) + sum(
            xrbuf[pl.ds(k * BATCH + p * slice_rows, slice_rows), :]
            * vec(VEC_CW + k // 2, (k % 2) * W_LRU, W_LRU)
            for k in range(CONV_WIDTH))
        if p == TIME_SLICES - 1:
            xrbuf[pl.ds(0, HIST_ROWS), :] = xrbuf[pl.ds(ROWS, HIST_ROWS), :]

    def lru_gates(p, hf):
        rows = rows_of(p)
        c0 = hf * MXU_TILE
        cv = convbuf[rows, c0:c0 + MXU_TILE]
        ri = _bdot(cv, wg_ref[hf])
        r = jax.nn.sigmoid(ri[:, 0:MXU_TILE] + vec(VEC_CB_BA, W_LRU + c0, MXU_TILE))
        ig = jax.nn.sigmoid(ri[:, MXU_TILE:] + vec(VEC_BX_CLOG, c0, MXU_TILE))
        log_a = vec(VEC_BX_CLOG, W_LRU + c0, MXU_TILE) * r
        a = jnp.exp(log_a)
        abuf[rows, c0:c0 + MXU_TILE] = a
        one_minus_a2 = -jnp.tanh(log_a) * (a * a + 1.0)
        gbuf[rows, c0:c0 + MXU_TILE] = jnp.sqrt(one_minus_a2) * (ig * cv)

    def lru_scan(p):
        hstate = lrustate[...]
        for t in range(p * SLICE_STEPS, (p + 1) * SLICE_STEPS):
            rows = slice(t * BATCH, (t + 1) * BATCH)
            hstate = abuf[rows, :] * hstate + gbuf[rows, :]
            hlru[rows, :] = hstate
        lrustate[...] = hstate

    def lru_out(p):
        rows = rows_of(p)
        out = hlru[rows, :] * jax.nn.gelu(yrbuf[rows, :])
        mixbuf[rows, W_S5:] = _rms(out, vec(VEC_S5G_LRUG, W_S5, W_LRU)).astype(bf16)

    def out_proj(p):
        rows = rows_of(p)
        xtb[rows, :] = jnp.dot(mixbuf[rows, :], wout_ref[...], preferred_element_type=f32)

    def scatter(p):
        t0 = p * SLICE_STEPS
        for t in range(t0, t0 + SLICE_STEPS):
            row = xtb[t * BATCH:(t + 1) * BATCH, :]
            for j in range(LANE_CHUNKS):
                osb[j, pl.ds(t, BATCH, stride=STAGE_SEG), :] = row[:, j * LANES:(j + 1) * LANES]
        for b in range(BATCH):
            seg = slice(b * STAGE_SEG + t0, b * STAGE_SEG + t0 + SLICE_STEPS)
            delta = jnp.concatenate([osb[j, seg, :] for j in range(LANE_CHUNKS)], axis=-1)
            rows = slice(b * TIME_CHUNK + t0, b * TIME_CHUNK + t0 + SLICE_STEPS)
            o2d[rows, :] = x2d[rows, :] + delta

    per_slice = dict(norm_stage=norm_stage, in_proj=in_proj, s5_out=s5_out, lru_conv=lru_conv,
                     lru_scan=lru_scan, lru_out=lru_out, out_proj=out_proj, scatter=scatter)
    per_half = dict(b_proj=b_proj, s5_scan=s5_scan, c_proj=c_proj, lru_gates=lru_gates)
    steps = {(name, p): (lambda f=f, p=p: f(p))
             for name, f in per_slice.items() for p in range(TIME_SLICES)}
    steps.update({(name, p, hf): (lambda f=f, p=p, hf=hf: f(p, hf))
                  for name, f in per_half.items() for p in range(TIME_SLICES)
                  for hf in range(S5_HALVES)})
    return steps


def _step_edges():
    edges = []
    for p in range(TIME_SLICES):
        edges += [(("norm_stage", p), ("in_proj", p)), (("in_proj", p), ("lru_conv", p)),
                  (("in_proj", p), ("lru_out", p)), (("in_proj", p), ("s5_out", p)),
                  (("lru_scan", p), ("lru_out", p)), (("s5_out", p), ("out_proj", p)),
                  (("lru_out", p), ("out_proj", p)), (("out_proj", p), ("scatter", p))]
        for hf in range(S5_HALVES):
            edges += [(("in_proj", p), ("b_proj", p, hf)), (("b_proj", p, hf), ("s5_scan", p, hf)),
                      (("s5_scan", p, hf), ("c_proj", p, hf)), (("c_proj", p, hf), ("s5_out", p)),
                      (("lru_conv", p), ("lru_gates", p, hf)),
                      (("lru_gates", p, hf), ("lru_scan", p))]
    edges += [(("in_proj", 0), ("lru_conv", 1)), (("lru_conv", 0), ("lru_conv", 1)),
              (("lru_scan", 0), ("lru_scan", 1))]
    edges += [(("s5_scan", 0, hf), ("s5_scan", 1, hf)) for hf in range(S5_HALVES)]
    return edges


assert TIME_SLICES == 2 and S5_HALVES == 2
_MIXER_ORDER = (
    ("norm_stage", 1), ("in_proj", 1), ("norm_stage", 0), ("in_proj", 0), ("lru_conv", 0),
    ("lru_gates", 0, 1), ("lru_gates", 0, 0), ("b_proj", 0, 1), ("lru_conv", 1), ("b_proj", 1, 1),
    ("lru_scan", 0), ("s5_scan", 0, 1), ("lru_gates", 1, 0), ("b_proj", 0, 0), ("lru_out", 0),
    ("c_proj", 0, 1), ("s5_scan", 0, 0), ("lru_gates", 1, 1), ("lru_scan", 1), ("c_proj", 0, 0),
    ("b_proj", 1, 0), ("s5_scan", 1, 1), ("c_proj", 1, 1), ("s5_out", 0), ("s5_scan", 1, 0),
    ("c_proj", 1, 0), ("lru_out", 1), ("out_proj", 0), ("s5_out", 1), ("scatter", 0),
    ("out_proj", 1), ("scatter", 1),
)
_POSITION = {step: k for k, step in enumerate(_MIXER_ORDER)}
assert len(_POSITION) == len(_MIXER_ORDER)
assert all(_POSITION[a] < _POSITION[b] for a, b in _step_edges())


def _mixer_kernel(x_ref, vec_ref, win_ref, wb_ref, wc_ref, wglu_ref, wg_ref, wout_ref,
                  o_ref,
                  xs, osb, xtb, hbuf, ubuf, xrbuf, yrbuf, bubuf, stbuf, ybuf, s5state,
                  convbuf, abuf, gbuf, hlru, lrustate, mixbuf):
    @pl.when(pl.program_id(0) == 0)
    def _():
        s5state[...] = jnp.zeros_like(s5state)
        lrustate[...] = jnp.zeros_like(lrustate)
        xrbuf[pl.ds(0, HIST_ROWS), :] = jnp.zeros((HIST_ROWS, W_LRU), jnp.float32)

    steps = _mixer_phases(
        x_ref.reshape(ROWS, D_MODEL), o_ref.reshape(ROWS, D_MODEL),
        vec_ref, win_ref, wb_ref, wc_ref, wglu_ref, wg_ref, wout_ref,
        xs, osb, xtb, hbuf, ubuf, xrbuf, yrbuf, bubuf, stbuf, ybuf, s5state,
        convbuf, abuf, gbuf, hlru, lrustate, mixbuf)
    assert sorted(steps) == sorted(_MIXER_ORDER)
    for name in _MIXER_ORDER:
        steps[name]()


def _ffn_kernel(x_ref, vec_ref, wgate_ref, wup_ref, wdown_ref, o_ref, actbuf):
    f32 = jnp.float32
    bf16 = jnp.bfloat16
    for part in range(FFN_PARTS):
        rows = slice(part * FFN_ROWS // FFN_PARTS, (part + 1) * FFN_ROWS // FFN_PARTS)
        x = x_ref[rows, :]
        h2 = _rms(x, vec_ref[0:1, :]).astype(bf16)
        for c in range(D_FF // FF_CHUNK):
            c0 = c * FF_CHUNK
            gate = jnp.dot(h2, wgate_ref[:, c0:c0 + FF_CHUNK], preferred_element_type=f32)
            up = jnp.dot(h2, wup_ref[:, c0:c0 + FF_CHUNK], preferred_element_type=f32)
            actbuf[rows, c0:c0 + FF_CHUNK] = (jax.nn.silu(gate) * up).astype(bf16)
        x2 = x + jnp.dot(actbuf[rows, :], wdown_ref[...], preferred_element_type=f32)
        o_ref[rows, :] = _rms(x2, vec_ref[1:2, :])


def _const_spec(shape):
    nd = len(shape)
    return pl.BlockSpec(shape, lambda i, _nd=nd: (0,) * _nd, pipeline_mode=pl.Buffered(1))


def _prepare_s5(lam_re, lam_im, log_step, b_re, b_im, c_re, c_im):
    f32 = jnp.float32
    la = jnp.minimum(lam_re.astype(f32), -1e-4)
    lb = lam_im.astype(f32)
    step = jnp.exp(log_step.astype(f32))[:, None]
    mag = jnp.exp(la * step)
    lbar_re = mag * jnp.cos(lb * step)
    lbar_im = mag * jnp.sin(lb * step)
    den = la * la + lb * lb
    q_re = (((lbar_re - 1.0) * la + lbar_im * lb) / den)[..., None]
    q_im = ((lbar_im * la - (lbar_re - 1.0) * lb) / den)[..., None]
    bre = b_re.astype(f32)
    bim = b_im.astype(f32)
    bbar = jnp.stack([q_re * bre - q_im * bim, q_re * bim + q_im * bre])
    cc = jnp.stack([c_re.astype(f32), -c_im.astype(f32)])
    groups = jnp.arange(HALF_GROUPS)[:, None]
    xb = bbar.reshape(2, S5_HALVES, HALF_GROUPS, S5_STATE, S5_GROUP_CH).transpose(1, 2, 4, 0, 3)
    on_diag = jnp.arange(HALF_STATES)[None, :] // S5_STATE == groups
    wb = jnp.where(on_diag[None, :, None, None, :], jnp.tile(xb, (1, 1, 1, 1, HALF_GROUPS)), 0.0)
    wb = wb.reshape(S5_HALVES, MXU_TILE, 2 * HALF_STATES)
    xc = cc.reshape(2, S5_HALVES, HALF_GROUPS, S5_GROUP_CH, S5_STATE).transpose(1, 0, 2, 4, 3)
    on_diag = jnp.arange(MXU_TILE)[None, :] // S5_GROUP_CH == groups
    wc = jnp.where(on_diag[None, None, :, None, :], jnp.tile(xc, (1, 1, 1, 1, HALF_GROUPS)), 0.0)
    wc = wc.reshape(S5_HALVES, 2 * HALF_STATES, MXU_TILE)
    lam_rows = jnp.stack([lbar_re.reshape(S5_HALVES, HALF_STATES),
                          lbar_im.reshape(S5_HALVES, HALF_STATES)], axis=1)
    return wb.astype(jnp.bfloat16), wc.astype(jnp.bfloat16), lam_rows


def _prepare_lru_gates(w_a, w_x):
    f32 = jnp.float32
    ww = jnp.stack([w_a.astype(f32), w_x.astype(f32)]).reshape(
        2, 2, LRU_HALF_HEADS, LRU_HEAD_DIM, LRU_HEAD_DIM).transpose(1, 2, 3, 0, 4)
    on_diag = (jnp.arange(MXU_TILE)[None, :] // LRU_HEAD_DIM
               == jnp.arange(LRU_HALF_HEADS)[:, None])
    wg = jnp.where(on_diag[None, :, None, None, :], jnp.tile(ww, (1, 1, 1, 1, LRU_HALF_HEADS)), 0.0)
    return wg.reshape(2, MXU_TILE, 2 * MXU_TILE).astype(jnp.bfloat16)


def _mixer(x, norm1_g, w_in, s5_lambda_re, s5_lambda_im, s5_log_step, s5_b_re, s5_b_im,
           s5_c_re, s5_c_im, s5_d, s5_w_glu, s5_b_glu, lru_conv_w, lru_conv_b, lru_w_a, lru_b_a,
           lru_w_x, lru_b_x, lru_lambda, s5_out_g, lru_out_g, w_out):
    f32 = jnp.float32
    bf16 = jnp.bfloat16
    wb, wc, lam_rows = _prepare_s5(s5_lambda_re, s5_lambda_im, s5_log_step, s5_b_re, s5_b_im,
                                   s5_c_re, s5_c_im)
    wg = _prepare_lru_gates(lru_w_a, lru_w_x)
    clog = RG_C * jax.nn.log_sigmoid(lru_lambda.astype(f32))
    used = [norm1_g, s5_d, s5_b_glu, lru_conv_b, lru_b_a, lru_b_x, clog, s5_out_g, lru_out_g,
            lru_conv_w, lam_rows]
    flat = [v.astype(f32).reshape(-1) for v in used]
    pad = VEC_ROWS * D_MODEL - sum(v.size for v in flat)
    vecs = jnp.concatenate(flat + [jnp.zeros((pad,), f32)]).reshape(VEC_ROWS, D_MODEL)

    inputs = [x, vecs, w_in.astype(bf16), wb, wc, s5_w_glu.astype(bf16), wg, w_out.astype(bf16)]
    x_spec = pl.BlockSpec((BATCH, TIME_CHUNK, D_MODEL), lambda i: (0, i, 0))
    in_specs = [x_spec] + [_const_spec(a.shape) for a in inputs[1:]]
    scratch = [
        pltpu.VMEM((LANE_CHUNKS, STAGE_ROWS, LANES), f32),
        pltpu.VMEM((LANE_CHUNKS, STAGE_ROWS, LANES), f32),
        pltpu.VMEM((ROWS, D_MODEL), f32),
        pltpu.VMEM((ROWS, D_MODEL), bf16),
        pltpu.VMEM((ROWS, W_S5), f32),
        pltpu.VMEM((ROWS + HIST_ROWS, W_LRU), f32),
        pltpu.VMEM((ROWS, W_LRU), f32),
        pltpu.VMEM((S5_HALVES, ROWS, 2 * HALF_STATES), f32),
        pltpu.VMEM((S5_HALVES, ROWS, 2 * HALF_STATES), bf16),
        pltpu.VMEM((ROWS, W_S5), f32),
        pltpu.VMEM((BATCH, 2 * S5_GROUPS * S5_STATE), f32),
        pltpu.VMEM((ROWS, W_LRU), f32),
        pltpu.VMEM((ROWS, W_LRU), f32),
        pltpu.VMEM((ROWS, W_LRU), f32),
        pltpu.VMEM((ROWS, W_LRU), f32),
        pltpu.VMEM((BATCH, W_LRU), f32),
        pltpu.VMEM((ROWS, D_MODEL), bf16),
    ]
    return pl.pallas_call(
        _mixer_kernel,
        grid=(NUM_CHUNKS,),
        in_specs=in_specs,
        out_specs=x_spec,
        out_shape=jax.ShapeDtypeStruct((BATCH, SEQ, D_MODEL), f32),
        scratch_shapes=scratch,
        compiler_params=pltpu.CompilerParams(
            dimension_semantics=("arbitrary",), vmem_limit_bytes=VMEM_LIMIT_BYTES),
        name="mixer",
    )(*inputs)


def _ffn(x1, norm2_g, w_gate, w_up, w_down, final_g):
    f32 = jnp.float32
    bf16 = jnp.bfloat16
    rows = x1.shape[0]
    row_spec = pl.BlockSpec((FFN_ROWS, D_MODEL), lambda i: (i, 0))
    vecs = jnp.concatenate([norm2_g.astype(f32), final_g.astype(f32),
                            jnp.zeros(((SUBLANES - 2) * D_MODEL,), f32)]).reshape(SUBLANES, D_MODEL)
    inputs = [x1, vecs, w_gate.astype(bf16), w_up.astype(bf16), w_down.astype(bf16)]
    return pl.pallas_call(
        _ffn_kernel,
        grid=(rows // FFN_ROWS,),
        in_specs=[row_spec] + [_const_spec(a.shape) for a in inputs[1:]],
        out_specs=row_spec,
        out_shape=jax.ShapeDtypeStruct((rows, D_MODEL), f32),
        scratch_shapes=[pltpu.VMEM((FFN_ROWS, D_FF), bf16)],
        compiler_params=pltpu.CompilerParams(
            dimension_semantics=("arbitrary",), vmem_limit_bytes=VMEM_LIMIT_BYTES),
        name="ffn",
    )(*inputs)


def kernel(x, norm1_g, w_in, s5_lambda_re, s5_lambda_im, s5_log_step, s5_b_re, s5_b_im, s5_c_re, s5_c_im, s5_d, s5_w_glu, s5_b_glu, lru_conv_w, lru_conv_b, lru_w_a, lru_b_a, lru_w_x, lru_b_x, lru_lambda, s5_out_g, lru_out_g, w_out, norm2_g, w_gate, w_up, w_down, final_g):
    x1 = _mixer(
        x, norm1_g[0], w_in[0], s5_lambda_re[0], s5_lambda_im[0], s5_log_step[0], s5_b_re[0],
        s5_b_im[0], s5_c_re[0], s5_c_im[0], s5_d[0], s5_w_glu[0], s5_b_glu[0], lru_conv_w[0],
        lru_conv_b[0], lru_w_a[0], lru_b_a[0], lru_w_x[0], lru_b_x[0], lru_lambda[0],
        s5_out_g[0], lru_out_g[0], w_out[0])
    out = _ffn(x1.reshape(BATCH * SEQ, D_MODEL), norm2_g[0], w_gate[0], w_up[0], w_down[0],
               final_g)
    return out.reshape(BATCH, SEQ, D_MODEL)
```

```python
import jax
import jax.numpy as jnp
from jax import lax
from jax.experimental import pallas as pl
from jax.experimental.pallas import tpu as pltpu

D_MODEL = 1024
BATCH = 16
SEQ = 2048
W_S5 = 512
S5_GROUP_CH = 16
S5_GROUPS = 32
S5_STATE = 64
W_LRU = 512
LRU_HEADS = 8
LRU_HEAD_DIM = 64
CONV_WIDTH = 4
RG_C = 8.0
D_FF = 2816
EPS = 1e-6

MXU_TILE = 256
S5_HALVES = W_S5 // MXU_TILE
HALF_GROUPS = S5_GROUPS // S5_HALVES
HALF_STATES = HALF_GROUPS * S5_STATE
SCAN_SLAB = 512
LRU_HALF_HEADS = LRU_HEADS // 2

TIME_CHUNK = 32
NUM_CHUNKS = SEQ // TIME_CHUNK
ROWS = TIME_CHUNK * BATCH
HIST_ROWS = (CONV_WIDTH - 1) * BATCH
LANES = 128
LANE_CHUNKS = D_MODEL // LANES
SUBLANES = 8
STAGE_SEG = TIME_CHUNK + SUBLANES
STAGE_ROWS = BATCH * STAGE_SEG
TIME_SLICES = 2
SLICE_STEPS = TIME_CHUNK // TIME_SLICES
FF_CHUNK = 256
FFN_ROWS = 1024
FFN_PARTS = 4
VMEM_LIMIT_BYTES = 56 * 1024 * 1024
VEC_G1 = 0
VEC_D_BGLU = 1
VEC_CB_BA = 2
VEC_BX_CLOG = 3
VEC_S5G_LRUG = 4
VEC_CW = 5
VEC_LAM = 7
VEC_ROWS = 16


def _rms(x, g):
    ms = jnp.mean(x * x, axis=-1, keepdims=True)
    return x * lax.rsqrt(ms + EPS) * g


def _bdot(a, b):
    return jnp.dot(a.astype(jnp.bfloat16), b, preferred_element_type=jnp.float32)


def _mixer_phases(x2d, o2d, vec_ref, win_ref, wb_ref, wc_ref, wglu_ref, wg_ref,
                  wout_ref, xs, osb, xtb, hbuf, ubuf, xrbuf, yrbuf, bubuf, stbuf, ybuf, s5state,
                  convbuf, abuf, gbuf, hlru, lrustate, mixbuf):
    f32 = jnp.float32
    bf16 = jnp.bfloat16
    slice_rows = SLICE_STEPS * BATCH

    def vec(row, col0, width):
        return vec_ref[row:row + 1, col0:col0 + width]

    def rows_of(p):
        return slice(p * slice_rows, (p + 1) * slice_rows)

    def norm_stage(p):
        t0 = p * SLICE_STEPS
        g1 = vec(VEC_G1, 0, D_MODEL)
        for b in range(BATCH):
            hn = _rms(x2d[b * TIME_CHUNK + t0:b * TIME_CHUNK + t0 + SLICE_STEPS, :], g1)
            for j in range(LANE_CHUNKS):
                xs[j, b * STAGE_SEG + t0:b * STAGE_SEG + t0 + SLICE_STEPS, :] = (
                    hn[:, j * LANES:(j + 1) * LANES])

    def norm_gather(p):
        t0 = p * SLICE_STEPS
        for t in range(t0, t0 + SLICE_STEPS):
            ht = jnp.concatenate(
                [xs[j, pl.ds(t, BATCH, stride=STAGE_SEG), :] for j in range(LANE_CHUNKS)],
                axis=-1)
            hbuf[t * BATCH:(t + 1) * BATCH, :] = ht.astype(bf16)

    def in_proj(p, part):
        rows = rows_of(p)
        h = hbuf[rows, :]
        if part == 0:
            ubuf[rows, :] = jnp.dot(h, win_ref[:, 0:W_S5], preferred_element_type=f32)
        elif part == 1:
            xrbuf[pl.ds(HIST_ROWS + p * slice_rows, slice_rows), :] = jnp.dot(
                h, win_ref[:, W_S5:W_S5 + W_LRU], preferred_element_type=f32)
        else:
            yrbuf[rows, :] = jnp.dot(h, win_ref[:, W_S5 + W_LRU:], preferred_element_type=f32)

    def b_proj(p, hf):
        rows = rows_of(p)
        bubuf[hf, rows, :] = _bdot(ubuf[rows, hf * MXU_TILE:(hf + 1) * MXU_TILE], wb_ref[hf])

    def s5_scan(p, hf):
        for q in range(HALF_STATES // SCAN_SLAB):
            re0 = q * SCAN_SLAB
            im0 = HALF_STATES + q * SCAN_SLAB
            sbase = hf * 2 * HALF_STATES
            lr = jnp.broadcast_to(vec(VEC_LAM + 2 * hf, re0, SCAN_SLAB), (BATCH, SCAN_SLAB))
            li = jnp.broadcast_to(vec(VEC_LAM + 2 * hf + 1, re0, SCAN_SLAB), (BATCH, SCAN_SLAB))
            sre = s5state[:, sbase + re0:sbase + re0 + SCAN_SLAB]
            sim = s5state[:, sbase + im0:sbase + im0 + SCAN_SLAB]
            for t in range(p * SLICE_STEPS, (p + 1) * SLICE_STEPS):
                rows = slice(t * BATCH, (t + 1) * BATCH)
                bre = bubuf[hf, rows, re0:re0 + SCAN_SLAB]
                bim = bubuf[hf, rows, im0:im0 + SCAN_SLAB]
                sre, sim = lr * sre - li * sim + bre, lr * sim + li * sre + bim
                stbuf[hf, rows, re0:re0 + SCAN_SLAB] = sre.astype(bf16)
                stbuf[hf, rows, im0:im0 + SCAN_SLAB] = sim.astype(bf16)
            s5state[:, sbase + re0:sbase + re0 + SCAN_SLAB] = sre
            s5state[:, sbase + im0:sbase + im0 + SCAN_SLAB] = sim

    def c_proj(p, hf):
        rows = rows_of(p)
        ybuf[rows, hf * MXU_TILE:(hf + 1) * MXU_TILE] = jnp.dot(
            stbuf[hf, rows, :], wc_ref[hf], preferred_element_type=f32)

    def s5_out(p):
        rows = rows_of(p)
        y = ybuf[rows, :] + vec(VEC_D_BGLU, 0, W_S5) * ubuf[rows, :]
        z = jax.nn.gelu(y)
        z = z * jax.nn.sigmoid(_bdot(z, wglu_ref[...]) + vec(VEC_D_BGLU, W_S5, W_S5))
        mixbuf[rows, 0:W_S5] = _rms(z, vec(VEC_S5G_LRUG, 0, W_S5)).astype(bf16)

    def lru_conv(p):
        convbuf[rows_of(p), :] = vec(VEC_CB_BA, 0, W_LRU) + sum(
            xrbuf[pl.ds(k * BATCH + p * slice_rows, slice_rows), :]
            * vec(VEC_CW + k // 2, (k % 2) * W_LRU, W_LRU)
            for k in range(CONV_WIDTH))
        if p == TIME_SLICES - 1:
            xrbuf[pl.ds(0, HIST_ROWS), :] = xrbuf[pl.ds(ROWS, HIST_ROWS), :]

    def lru_gates(p, hf):
        rows = rows_of(p)
        c0 = hf * MXU_TILE
        cv = convbuf[rows, c0:c0 + MXU_TILE]
        ri = _bdot(cv, wg_ref[hf])
        r = jax.nn.sigmoid(ri[:, 0:MXU_TILE] + vec(VEC_CB_BA, W_LRU + c0, MXU_TILE))
        ig = jax.nn.sigmoid(ri[:, MXU_TILE:] + vec(VEC_BX_CLOG, c0, MXU_TILE))
        log_a = vec(VEC_BX_CLOG, W_LRU + c0, MXU_TILE) * r
        a = jnp.exp(log_a)
        abuf[rows, c0:c0 + MXU_TILE] = a
        one_minus_a2 = -jnp.tanh(log_a) * (a * a + 1.0)
        gbuf[rows, c0:c0 + MXU_TILE] = jnp.sqrt(one_minus_a2) * (ig * cv)

    def lru_scan(p):
        hstate = lrustate[...]
        for t in range(p * SLICE_STEPS, (p + 1) * SLICE_STEPS):
            rows = slice(t * BATCH, (t + 1) * BATCH)
            hstate = abuf[rows, :] * hstate + gbuf[rows, :]
            hlru[rows, :] = hstate
        lrustate[...] = hstate

    def lru_out(p):
        rows = rows_of(p)
        out = hlru[rows, :] * jax.nn.gelu(yrbuf[rows, :])
        mixbuf[rows, W_S5:] = _rms(out, vec(VEC_S5G_LRUG, W_S5, W_LRU)).astype(bf16)

    def out_proj(p):
        rows = rows_of(p)
        xtb[rows, :] = jnp.dot(mixbuf[rows, :], wout_ref[...], preferred_element_type=f32)

    def scatter(p):
        t0 = p * SLICE_STEPS
        for t in range(t0, t0 + SLICE_STEPS):
            row = xtb[t * BATCH:(t + 1) * BATCH, :]
            for j in range(LANE_CHUNKS):
                osb[j, pl.ds(t, BATCH, stride=STAGE_SEG), :] = row[:, j * LANES:(j + 1) * LANES]

    def residual(p):
        t0 = p * SLICE_STEPS
        for b in range(BATCH):
            seg = slice(b * STAGE_SEG + t0, b * STAGE_SEG + t0 + SLICE_STEPS)
            delta = jnp.concatenate([osb[j, seg, :] for j in range(LANE_CHUNKS)], axis=-1)
            rows = slice(b * TIME_CHUNK + t0, b * TIME_CHUNK + t0 + SLICE_STEPS)
            o2d[rows, :] = x2d[rows, :] + delta

    per_slice = dict(norm_stage=norm_stage, norm_gather=norm_gather, s5_out=s5_out,
                     lru_conv=lru_conv, lru_scan=lru_scan, lru_out=lru_out, out_proj=out_proj,
                     scatter=scatter, residual=residual)
    per_half = dict(b_proj=b_proj, s5_scan=s5_scan, c_proj=c_proj, lru_gates=lru_gates)
    steps = {(name, p): (lambda f=f, p=p: f(p))
             for name, f in per_slice.items() for p in range(TIME_SLICES)}
    steps.update({(name, p, hf): (lambda f=f, p=p, hf=hf: f(p, hf))
                  for name, f in per_half.items() for p in range(TIME_SLICES)
                  for hf in range(S5_HALVES)})
    steps.update({("in_proj", p, part): (lambda p=p, part=part: in_proj(p, part))
                  for p in range(TIME_SLICES) for part in range(3)})
    return steps


def _step_edges():
    edges = []
    for p in range(TIME_SLICES):
        edges += [(("norm_stage", p), ("norm_gather", p)), (("in_proj", p, 1), ("lru_conv", p)),
                  (("in_proj", p, 2), ("lru_out", p)), (("in_proj", p, 0), ("s5_out", p)),
                  (("lru_scan", p), ("lru_out", p)), (("s5_out", p), ("out_proj", p)),
                  (("lru_out", p), ("out_proj", p)), (("out_proj", p), ("scatter", p)),
                  (("scatter", p), ("residual", p))]
        edges += [(("norm_gather", p), ("in_proj", p, part)) for part in range(3)]
        for hf in range(S5_HALVES):
            edges += [(("in_proj", p, 0), ("b_proj", p, hf)), (("b_proj", p, hf), ("s5_scan", p, hf)),
                      (("s5_scan", p, hf), ("c_proj", p, hf)), (("c_proj", p, hf), ("s5_out", p)),
                      (("lru_conv", p), ("lru_gates", p, hf)),
                      (("lru_gates", p, hf), ("lru_scan", p))]
    edges += [(("in_proj", 0, 1), ("lru_conv", 1)), (("lru_conv", 0), ("lru_conv", 1)),
              (("lru_scan", 0), ("lru_scan", 1))]
    edges += [(("s5_scan", 0, hf), ("s5_scan", 1, hf)) for hf in range(S5_HALVES)]
    return edges


assert TIME_SLICES == 2 and S5_HALVES == 2
_MIXER_ORDER = (
    ("norm_stage", 1), ("norm_gather", 1), ("in_proj", 1, 1), ("in_proj", 1, 0), ("norm_stage", 0),
    ("norm_gather", 0), ("in_proj", 0, 0), ("in_proj", 0, 1), ("in_proj", 0, 2), ("lru_conv", 0),
    ("lru_gates", 0, 0), ("lru_gates", 0, 1), ("b_proj", 0, 1), ("b_proj", 1, 1), ("lru_scan", 0),
    ("b_proj", 0, 0), ("s5_scan", 0, 1), ("lru_conv", 1), ("lru_gates", 1, 0), ("c_proj", 0, 1),
    ("s5_scan", 0, 0), ("lru_gates", 1, 1), ("lru_scan", 1), ("in_proj", 1, 2), ("c_proj", 0, 0),
    ("b_proj", 1, 0), ("s5_scan", 1, 1), ("c_proj", 1, 1), ("lru_out", 0), ("s5_out", 0),
    ("s5_scan", 1, 0), ("c_proj", 1, 0), ("lru_out", 1), ("out_proj", 0), ("s5_out", 1),
    ("out_proj", 1), ("scatter", 0), ("scatter", 1), ("residual", 1), ("residual", 0),
)
_POSITION = {step: k for k, step in enumerate(_MIXER_ORDER)}
assert len(_POSITION) == len(_MIXER_ORDER)
assert all(_POSITION[a] < _POSITION[b] for a, b in _step_edges())


def _mixer_kernel(x_ref, vec_ref, win_ref, wb_ref, wc_ref, wglu_ref, wg_ref, wout_ref,
                  o_ref,
                  xs, osb, xtb, hbuf, ubuf, xrbuf, yrbuf, bubuf, stbuf, ybuf, s5state,
                  convbuf, abuf, gbuf, hlru, lrustate, mixbuf):
    @pl.when(pl.program_id(0) == 0)
    def _():
        s5state[...] = jnp.zeros_like(s5state)
        lrustate[...] = jnp.zeros_like(lrustate)
        xrbuf[pl.ds(0, HIST_ROWS), :] = jnp.zeros((HIST_ROWS, W_LRU), jnp.float32)

    steps = _mixer_phases(
        x_ref.reshape(ROWS, D_MODEL), o_ref.reshape(ROWS, D_MODEL),
        vec_ref, win_ref, wb_ref, wc_ref, wglu_ref, wg_ref, wout_ref,
        xs, osb, xtb, hbuf, ubuf, xrbuf, yrbuf, bubuf, stbuf, ybuf, s5state,
        convbuf, abuf, gbuf, hlru, lrustate, mixbuf)
    assert sorted(steps) == sorted(_MIXER_ORDER)
    for name in _MIXER_ORDER:
        steps[name]()


def _ffn_kernel(x_ref, vec_ref, wgate_ref, wup_ref, wdown_ref, o_ref, actbuf):
    f32 = jnp.float32
    bf16 = jnp.bfloat16
    for part in range(FFN_PARTS):
        rows = slice(part * FFN_ROWS // FFN_PARTS, (part + 1) * FFN_ROWS // FFN_PARTS)
        x = x_ref[rows, :]
        h2 = _rms(x, vec_ref[0:1, :]).astype(bf16)
        for c in range(D_FF // FF_CHUNK):
            c0 = c * FF_CHUNK
            gate = jnp.dot(h2, wgate_ref[:, c0:c0 + FF_CHUNK], preferred_element_type=f32)
            up = jnp.dot(h2, wup_ref[:, c0:c0 + FF_CHUNK], preferred_element_type=f32)
            actbuf[rows, c0:c0 + FF_CHUNK] = (jax.nn.silu(gate) * up).astype(bf16)
        x2 = x + jnp.dot(actbuf[rows, :], wdown_ref[...], preferred_element_type=f32)
        o_ref[rows, :] = _rms(x2, vec_ref[1:2, :])


def _const_spec(shape):
    nd = len(shape)
    return pl.BlockSpec(shape, lambda i, _nd=nd: (0,) * _nd, pipeline_mode=pl.Buffered(1))


def _prepare_s5(lam_re, lam_im, log_step, b_re, b_im, c_re, c_im):
    f32 = jnp.float32
    la = jnp.minimum(lam_re.astype(f32), -1e-4)
    lb = lam_im.astype(f32)
    step = jnp.exp(log_step.astype(f32))[:, None]
    mag = jnp.exp(la * step)
    lbar_re = mag * jnp.cos(lb * step)
    lbar_im = mag * jnp.sin(lb * step)
    den = la * la + lb * lb
    q_re = (((lbar_re - 1.0) * la + lbar_im * lb) / den)[..., None]
    q_im = ((lbar_im * la - (lbar_re - 1.0) * lb) / den)[..., None]
    bre = b_re.astype(f32)
    bim = b_im.astype(f32)
    bbar = jnp.stack([q_re * bre - q_im * bim, q_re * bim + q_im * bre])
    cc = jnp.stack([c_re.astype(f32), -c_im.astype(f32)])
    groups = jnp.arange(HALF_GROUPS)[:, None]
    xb = bbar.reshape(2, S5_HALVES, HALF_GROUPS, S5_STATE, S5_GROUP_CH).transpose(1, 2, 4, 0, 3)
    on_diag = jnp.arange(HALF_STATES)[None, :] // S5_STATE == groups
    wb = jnp.where(on_diag[None, :, None, None, :], jnp.tile(xb, (1, 1, 1, 1, HALF_GROUPS)), 0.0)
    wb = wb.reshape(S5_HALVES, MXU_TILE, 2 * HALF_STATES)
    xc = cc.reshape(2, S5_HALVES, HALF_GROUPS, S5_GROUP_CH, S5_STATE).transpose(1, 0, 2, 4, 3)
    on_diag = jnp.arange(MXU_TILE)[None, :] // S5_GROUP_CH == groups
    wc = jnp.where(on_diag[None, None, :, None, :], jnp.tile(xc, (1, 1, 1, 1, HALF_GROUPS)), 0.0)
    wc = wc.reshape(S5_HALVES, 2 * HALF_STATES, MXU_TILE)
    lam_rows = jnp.stack([lbar_re.reshape(S5_HALVES, HALF_STATES),
                          lbar_im.reshape(S5_HALVES, HALF_STATES)], axis=1)
    return wb.astype(jnp.bfloat16), wc.astype(jnp.bfloat16), lam_rows


def _prepare_lru_gates(w_a, w_x):
    f32 = jnp.float32
    ww = jnp.stack([w_a.astype(f32), w_x.astype(f32)]).reshape(
        2, 2, LRU_HALF_HEADS, LRU_HEAD_DIM, LRU_HEAD_DIM).transpose(1, 2, 3, 0, 4)
    on_diag = (jnp.arange(MXU_TILE)[None, :] // LRU_HEAD_DIM
               == jnp.arange(LRU_HALF_HEADS)[:, None])
    wg = jnp.where(on_diag[None, :, None, None, :], jnp.tile(ww, (1, 1, 1, 1, LRU_HALF_HEADS)), 0.0)
    return wg.reshape(2, MXU_TILE, 2 * MXU_TILE).astype(jnp.bfloat16)


def _mixer(x, norm1_g, w_in, s5_lambda_re, s5_lambda_im, s5_log_step, s5_b_re, s5_b_im,
           s5_c_re, s5_c_im, s5_d, s5_w_glu, s5_b_glu, lru_conv_w, lru_conv_b, lru_w_a, lru_b_a,
           lru_w_x, lru_b_x, lru_lambda, s5_out_g, lru_out_g, w_out):
    f32 = jnp.float32
    bf16 = jnp.bfloat16
    wb, wc, lam_rows = _prepare_s5(s5_lambda_re, s5_lambda_im, s5_log_step, s5_b_re, s5_b_im,
                                   s5_c_re, s5_c_im)
    wg = _prepare_lru_gates(lru_w_a, lru_w_x)
    clog = RG_C * jax.nn.log_sigmoid(lru_lambda.astype(f32))
    used = [norm1_g, s5_d, s5_b_glu, lru_conv_b, lru_b_a, lru_b_x, clog, s5_out_g, lru_out_g,
            lru_conv_w, lam_rows]
    flat = [v.astype(f32).reshape(-1) for v in used]
    pad = VEC_ROWS * D_MODEL - sum(v.size for v in flat)
    vecs = jnp.concatenate(flat + [jnp.zeros((pad,), f32)]).reshape(VEC_ROWS, D_MODEL)

    inputs = [x, vecs, w_in.astype(bf16), wb, wc, s5_w_glu.astype(bf16), wg, w_out.astype(bf16)]
    x_spec = pl.BlockSpec((BATCH, TIME_CHUNK, D_MODEL), lambda i: (0, i, 0))
    in_specs = [x_spec] + [_const_spec(a.shape) for a in inputs[1:]]
    scratch = [
        pltpu.VMEM((LANE_CHUNKS, STAGE_ROWS, LANES), f32),
        pltpu.VMEM((LANE_CHUNKS, STAGE_ROWS, LANES), f32),
        pltpu.VMEM((ROWS, D_MODEL), f32),
        pltpu.VMEM((ROWS, D_MODEL), bf16),
        pltpu.VMEM((ROWS, W_S5), f32),
        pltpu.VMEM((ROWS + HIST_ROWS, W_LRU), f32),
        pltpu.VMEM((ROWS, W_LRU), f32),
        pltpu.VMEM((S5_HALVES, ROWS, 2 * HALF_STATES), f32),
        pltpu.VMEM((S5_HALVES, ROWS, 2 * HALF_STATES), bf16),
        pltpu.VMEM((ROWS, W_S5), f32),
        pltpu.VMEM((BATCH, 2 * S5_GROUPS * S5_STATE), f32),
        pltpu.VMEM((ROWS, W_LRU), f32),
        pltpu.VMEM((ROWS, W_LRU), f32),
        pltpu.VMEM((ROWS, W_LRU), f32),
        pltpu.VMEM((ROWS, W_LRU), f32),
        pltpu.VMEM((BATCH, W_LRU), f32),
        pltpu.VMEM((ROWS, D_MODEL), bf16),
    ]
    return pl.pallas_call(
        _mixer_kernel,
        grid=(NUM_CHUNKS,),
        in_specs=in_specs,
        out_specs=x_spec,
        out_shape=jax.ShapeDtypeStruct((BATCH, SEQ, D_MODEL), f32),
        scratch_shapes=scratch,
        compiler_params=pltpu.CompilerParams(
            dimension_semantics=("arbitrary",), vmem_limit_bytes=VMEM_LIMIT_BYTES),
        name="mixer",
    )(*inputs)


def _ffn(x1, norm2_g, w_gate, w_up, w_down, final_g):
    f32 = jnp.float32
    bf16 = jnp.bfloat16
    rows = x1.shape[0]
    row_spec = pl.BlockSpec((FFN_ROWS, D_MODEL), lambda i: (i, 0))
    vecs = jnp.concatenate([norm2_g.astype(f32), final_g.astype(f32),
                            jnp.zeros(((SUBLANES - 2) * D_MODEL,), f32)]).reshape(SUBLANES, D_MODEL)
    inputs = [x1, vecs, w_gate.astype(bf16), w_up.astype(bf16), w_down.astype(bf16)]
    return pl.pallas_call(
        _ffn_kernel,
        grid=(rows // FFN_ROWS,),
        in_specs=[row_spec] + [_const_spec(a.shape) for a in inputs[1:]],
        out_specs=row_spec,
        out_shape=jax.ShapeDtypeStruct((rows, D_MODEL), f32),
        scratch_shapes=[pltpu.VMEM((FFN_ROWS, D_FF), bf16)],
        compiler_params=pltpu.CompilerParams(
            dimension_semantics=("arbitrary",), vmem_limit_bytes=VMEM_LIMIT_BYTES),
        name="ffn",
    )(*inputs)


def kernel(x, norm1_g, w_in, s5_lambda_re, s5_lambda_im, s5_log_step, s5_b_re, s5_b_im, s5_c_re, s5_c_im, s5_d, s5_w_glu, s5_b_glu, lru_conv_w, lru_conv_b, lru_w_a, lru_b_a, lru_w_x, lru_b_x, lru_lambda, s5_out_g, lru_out_g, w_out, norm2_g, w_gate, w_up, w_down, final_g):
    x1 = _mixer(
        x, norm1_g[0], w_in[0], s5_lambda_re[0], s5_lambda_im[0], s5_log_step[0], s5_b_re[0],
        s5_b_im[0], s5_c_re[0], s5_c_im[0], s5_d[0], s5_w_glu[0], s5_b_glu[0], lru_conv_w[0],
        lru_conv_b[0], lru_w_a[0], lru_b_a[0], lru_w_x[0], lru_b_x[0], lru_lambda[0],
        s5_out_g[0], lru_out_g[0], w_out[0])
    out = _ffn(x1.reshape(BATCH * SEQ, D_MODEL), norm2_g[0], w_gate[0], w_up[0], w_down[0],
               final_g)
    return out.reshape(BATCH, SEQ, D_MODEL)
```
